```python
import jax, jax.numpy as jnp
from jax import lax
import numpy as np

D_MODEL = 4096
BATCH = 2
SEQ = 8192
DEPTH = 2

D_MIX = D_MODEL
HEAD_DIM = 128
W_A = 3 * D_MIX // 8
W_B = 5 * D_MIX // 16
W_C = D_MIX - W_A - W_B
H_A = W_A // HEAD_DIM
H_B = W_B // HEAD_DIM
CONV_WIDTH = 31
CHUNK = 128
POOL_WINDOWS = (2, 4, 8, 16)
N_POOL_GROUPS = len(POOL_WINDOWS)
C_GROUP = W_C // N_POOL_GROUPS
EPS = 1e-6
COL_WIDTHS = (W_A, W_A, W_A, W_B, W_B, W_B, W_C, W_C)
P_IN = 3 * W_A + 3 * W_B + 2 * W_C
SPLITS = (W_A, 2 * W_A, 3 * W_A, 3 * W_A + W_B, 3 * W_A + 2 * W_B, 3 * W_A + 3 * W_B, 3 * W_A + 3 * W_B + W_C)

kernel_name = "hybrid_conv_gmlp_pool_parallel_heads"


def _rmsnorm(x, g):
    xf = x.astype(jnp.float32)
    r = lax.rsqrt(jnp.mean(xf * xf, axis=-1, keepdims=True) + EPS)
    return (xf * r * g.astype(jnp.float32)).astype(x.dtype)


def _layernorm(x, g, b):
    xf = x.astype(jnp.float32)
    mu = jnp.mean(xf, axis=-1, keepdims=True)
    var = jnp.mean(jnp.square(xf - mu), axis=-1, keepdims=True)
    y = (xf - mu) * lax.rsqrt(var + EPS) * g.astype(jnp.float32) + b.astype(jnp.float32)
    return y.astype(x.dtype)


def _conformer_conv_branch(a_val, a_glu, conv_w, conv_b, ln_g, ln_b, pw):
    h = a_val * jax.nn.sigmoid(a_glu)
    h = lax.conv_general_dilated(
        h, conv_w[:, None, :].astype(h.dtype), window_strides=(1,),
        padding=[(CONV_WIDTH - 1, 0)],
        dimension_numbers=("NWC", "WIO", "NWC"),
        feature_group_count=h.shape[-1]) + conv_b
    h = jax.nn.silu(_layernorm(h, ln_g, ln_b))
    return h @ pw


def _gmlp_chunk_branch(u, v, ln_g, ln_b, ws, bias):
    b, s, c = v.shape
    u = jax.nn.gelu(u)
    v = _layernorm(jax.nn.gelu(v), ln_g, ln_b)
    vc = v.reshape(b, s // CHUNK, CHUNK, H_B, HEAD_DIM)
    ws_causal = jnp.tril(ws)
    sp = jnp.einsum('hpq,bnqhd->bnphd', ws_causal, vc) + bias.T[None, None, :, :, None]
    return u * sp.reshape(b, s, c)


def _pool_branch(c_in, c_w, c_scale):
    b, s, c = c_in.shape
    cf = c_in.astype(jnp.float32)
    cs = jnp.cumsum(cf, axis=1)
    count = jnp.arange(1, s + 1, dtype=jnp.float32)[:, None]
    diffs = []
    for gi, w in enumerate(POOL_WINDOWS):
        sl = slice(gi * C_GROUP, (gi + 1) * C_GROUP)
        csg = cs[..., sl]
        lag = jnp.pad(csg, ((0, 0), (w, 0), (0, 0)))[:, :s]
        mean = (csg - lag) / jnp.minimum(count, float(w))
        diffs.append(mean - cf[..., sl])
    d = jnp.stack(diffs, axis=2).astype(c_in.dtype)
    y = jnp.einsum('bsgc,gcd->bsgd', d, c_w).reshape(b, s, c)
    return y * c_scale


def setup_inputs(seed: int = 0) -> dict:
    key = jax.random.key(seed)
    ks = jax.random.split(key, 20)
    f32 = jnp.float32
    nrm = lambda k, shape, scale: jax.random.normal(k, shape, f32) * scale
    return {
        "x": nrm(ks[0], (BATCH, SEQ, D_MODEL), 1.0),
        "norm_g": 1.0 + nrm(ks[1], (DEPTH, D_MODEL), 0.02),
        "w_in": nrm(ks[2], (DEPTH, D_MODEL, P_IN), D_MODEL ** -0.5),
        "conv_w": nrm(ks[3], (DEPTH, CONV_WIDTH, W_A), CONV_WIDTH ** -0.5),
        "conv_b": nrm(ks[4], (DEPTH, W_A), 0.02),
        "a_ln_g": 1.0 + nrm(ks[5], (DEPTH, W_A), 0.02),
        "a_ln_b": nrm(ks[6], (DEPTH, W_A), 0.02),
        "a_pw": nrm(ks[7], (DEPTH, W_A, W_A), W_A ** -0.5),
        "b_ln_g": 1.0 + nrm(ks[8], (DEPTH, W_B), 0.02),
        "b_ln_b": nrm(ks[9], (DEPTH, W_B), 0.02),
        "b_ws": nrm(ks[10], (DEPTH, H_B, CHUNK, CHUNK), CHUNK ** -0.5),
        "b_bias": 1.0 + nrm(ks[11], (DEPTH, H_B, CHUNK), 0.02),
        "c_w": nrm(ks[12], (DEPTH, N_POOL_GROUPS, C_GROUP, C_GROUP), C_GROUP ** -0.5),
        "c_scale": 1.0 + nrm(ks[13], (DEPTH, W_C), 0.1),
        "w_out": nrm(ks[14], (DEPTH, D_MIX, D_MODEL), D_MIX ** -0.5),
        "final_g": 1.0 + nrm(ks[15], (D_MODEL,), 0.02),
    }


def reference(x, norm_g, w_in, conv_w, conv_b, a_ln_g, a_ln_b, a_pw, b_ln_g, b_ln_b,
              b_ws, b_bias, c_w, c_scale, w_out, final_g):
    for l in range(DEPTH):
        h = _rmsnorm(x, norm_g[l])
        proj = h @ w_in[l]
        a_val, a_glu, a_gate, b_u, b_v, b_gate, c_in, c_gate = jnp.split(proj, SPLITS, axis=-1)
        ya = _conformer_conv_branch(a_val, a_glu, conv_w[l], conv_b[l], a_ln_g[l], a_ln_b[l], a_pw[l])
        ya = ya * jax.nn.silu(a_gate)
        yb = _gmlp_chunk_branch(b_u, b_v, b_ln_g[l], b_ln_b[l], b_ws[l], b_bias[l])
        yb = yb * jax.nn.silu(b_gate)
        yc = _pool_branch(c_in, c_w[l], c_scale[l])
        yc = yc * jax.nn.silu(c_gate)
        y = jnp.concatenate([ya, yb, yc], axis=-1) @ w_out[l]
        x = x + y
    return _rmsnorm(x, final_g)
```

```python
import functools

import jax
import jax.numpy as jnp
from jax import lax
from jax.experimental import pallas as pl
from jax.experimental.pallas import tpu as pltpu

D_MODEL = 4096
HEAD_DIM = 128
W_A = 1536
W_B = 1280
W_C = 1280
H_B = W_B // HEAD_DIM
CONV_WIDTH = 31
CHUNK = 128
POOL_WINDOWS = (2, 4, 8, 16)
C_GROUP = W_C // len(POOL_WINDOWS)
P_IN = 3 * W_A + 3 * W_B + 2 * W_C
EPS = 1e-6

LANES = 128
SUBLANES = 8
MXU_DIM = 256
VMEM_LIMIT_BYTES = 56 * 1024 * 1024

TOKEN_TILE = 512
A_BLOCK = 256
SLAB_W = 3 * A_BLOCK
CONV_HALO = 32
POOL_HALO = 16
CONV_ROWS = 64
LN_ROWS = 128
OUT_TM = 512
OUT_TN = 512
NORM_TM = 256

OFF_A = 0
OFF_BV = 3 * W_A
OFF_BUG = OFF_BV + W_B
OFF_CIN = OFF_BUG + 2 * W_B
OFF_CG = OFF_CIN + W_C
C_HALF = W_C // 2


def _slab_schedule():
    s = []
    for j in range(W_A // A_BLOCK):
        s.append(("win", D_MODEL, OFF_A + j * SLAB_W, SLAB_W))
    for j in range(3):
        s.append(("apw", W_A, j * 512, 512))
    s.append(("win", D_MODEL, OFF_BV, 768))
    s.append(("win", D_MODEL, OFF_BV + 768, 512))
    for p in range(H_B // 2):
        s.append(("win", D_MODEL, OFF_BUG + p * 512, 512))
    s.append(("win", D_MODEL, OFF_CIN, 768))
    s.append(("win", D_MODEL, OFF_CIN + 768, 512))
    for j in range(2):
        s.append(("cw", C_HALF, j * C_HALF, C_HALF))
    s.append(("win", D_MODEL, OFF_CG, 768))
    s.append(("win", D_MODEL, OFF_CG + 768, 512))
    assert len(s) % 2 == 0
    return s


SLABS = _slab_schedule()


def _sigmoid(x):
    return 1.0 / (1.0 + jnp.exp(-x))


def _silu(x):
    return x * _sigmoid(x)


def _gelu(x):
    return jax.nn.gelu(x)


def _mixer_kernel(h_ref, win_hbm, apw_hbm, cw_hbm, convw_ref, convb_ref, alng_ref, alnb_ref,
                  blng_ref, blnb_ref, ws_ref, bbias_ref, cscale_ref,
                  out_ref,
                  wbuf, sem, ghalo, chalo, gext, gsh, ca, sgate, abf, gv, cext, dbf, yc,
                  *, tiles_per_seq):
    i = pl.program_id(0)
    n_tiles = pl.num_programs(0)
    T = h_ref.shape[0]
    srcs = {"win": win_hbm, "apw": apw_hbm, "cw": cw_hbm}

    def slab_copy(k):
        name, rows, off, width = SLABS[k]
        slot = k % 2
        return pltpu.make_async_copy(
            srcs[name].at[pl.ds(0, rows), pl.ds(off, width)],
            wbuf.at[slot, pl.ds(0, rows), pl.ds(0, width)],
            sem.at[slot])

    def slab_view(k):
        _, rows, _, width = SLABS[k]
        return wbuf.at[k % 2, pl.ds(0, rows), pl.ds(0, width)]

    @pl.when(i == 0)
    def _():
        slab_copy(0).start()

    state = {"k": 0}

    def next_slab():
        k = state["k"]
        state["k"] = k + 1
        if k + 1 < len(SLABS):
            slab_copy(k + 1).start()
        else:
            @pl.when(i + 1 < n_tiles)
            def _():
                slab_copy(0).start()
        slab_copy(k).wait()
        return slab_view(k)

    first_of_seq = (i % tiles_per_seq) == 0

    @pl.when(first_of_seq)
    def _():
        ghalo[...] = jnp.zeros_like(ghalo)
        chalo[...] = jnp.zeros_like(chalo)

    for j in range(W_A // A_BLOCK):
        cols = pl.ds(j * A_BLOCK, A_BLOCK)
        w = next_slab()
        r = jnp.dot(h_ref[...], w[...], preferred_element_type=jnp.float32)
        g = r[:, :A_BLOCK] * _sigmoid(r[:, A_BLOCK:2 * A_BLOCK])
        sgate[:, cols] = _silu(r[:, 2 * A_BLOCK:])
        gext[pl.ds(0, CONV_HALO), :] = ghalo[:, cols]
        gext[pl.ds(CONV_HALO, T), :] = g
        ghalo[:, cols] = g[T - CONV_HALO:, :]
        cwv = convw_ref[:, cols]
        cbv = convb_ref[:, cols]

        for b in range(1, SUBLANES):
            gsh[b - 1, pl.ds(0, T + CONV_HALO - SUBLANES), :] = gext[pl.ds(b, T + CONV_HALO - SUBLANES), :]

        def conv_chunk(c, carry):
            r0 = pl.multiple_of(c * CONV_ROWS, CONV_ROWS)
            acc = jnp.broadcast_to(cbv, (CONV_ROWS, A_BLOCK))
            for k in range(CONV_WIDTH):
                a, b = divmod(CONV_HALO - (CONV_WIDTH - 1) + k, SUBLANES)
                src = gext if b == 0 else gsh.at[b - 1]
                acc = acc + src[pl.ds(r0 + a * SUBLANES, CONV_ROWS), :] * cwv[k:k + 1, :]
            ca[pl.ds(r0, CONV_ROWS), cols] = acc
            return carry

        lax.fori_loop(0, T // CONV_ROWS, conv_chunk, 0)

    def a_ln_chunk(c, carry):
        r0 = pl.multiple_of(c * LN_ROWS, LN_ROWS)
        xx = ca[pl.ds(r0, LN_ROWS), :]
        mu = jnp.mean(xx, axis=-1, keepdims=True)
        xc = xx - mu
        var = jnp.mean(xc * xc, axis=-1, keepdims=True)
        y = xc * lax.rsqrt(var + EPS) * alng_ref[...] + alnb_ref[...]
        abf[pl.ds(r0, LN_ROWS), :] = _silu(y).astype(jnp.bfloat16)
        return carry

    lax.fori_loop(0, T // LN_ROWS, a_ln_chunk, 0)

    for j in range(3):
        cols = pl.ds(j * 512, 512)
        w = next_slab()
        ya = jnp.dot(abf[...], w[...], preferred_element_type=jnp.float32)
        out_ref[:, cols] = (ya * sgate[:, cols]).astype(out_ref.dtype)

    boff = 0
    for width in (768, 512):
        w = next_slab()
        r = jnp.dot(h_ref[...], w[...], preferred_element_type=jnp.float32)
        gv[:, pl.ds(boff, width)] = _gelu(r)
        boff += width

    def b_ln_chunk(c, carry):
        r0 = pl.multiple_of(c * LN_ROWS, LN_ROWS)
        xx = gv[pl.ds(r0, LN_ROWS), :]
        mu = jnp.mean(xx, axis=-1, keepdims=True)
        xc = xx - mu
        var = jnp.mean(xc * xc, axis=-1, keepdims=True)
        y = xc * lax.rsqrt(var + EPS) * blng_ref[...] + blnb_ref[...]
        dbf[pl.ds(r0, LN_ROWS), :] = y.astype(jnp.bfloat16)
        return carry

    lax.fori_loop(0, T // LN_ROWS, b_ln_chunk, 0)

    row_id = lax.broadcasted_iota(jnp.int32, (CHUNK, CHUNK), 0)
    col_id = lax.broadcasted_iota(jnp.int32, (CHUNK, CHUNK), 1)
    causal = col_id <= row_id
    for p in range(H_B // 2):
        w = next_slab()
        r = jnp.dot(h_ref[...], w[...], preferred_element_type=jnp.float32)
        m = _gelu(r[:, :2 * HEAD_DIM]) * _silu(r[:, 2 * HEAD_DIM:])
        for hh in range(2):
            head = 2 * p + hh
            wsm = jnp.where(causal, ws_ref[head], 0.0).astype(jnp.bfloat16)
            bcol = bbias_ref[head]
            ccols = pl.ds(head * HEAD_DIM, HEAD_DIM)
            for n in range(T // CHUNK):
                rows = pl.ds(n * CHUNK, CHUNK)
                sp = jnp.dot(wsm, dbf[rows, ccols], preferred_element_type=jnp.float32) + bcol
                mm = m[n * CHUNK:(n + 1) * CHUNK, hh * HEAD_DIM:(hh + 1) * HEAD_DIM]
                out_ref[rows, pl.ds(W_A + head * HEAD_DIM, HEAD_DIM)] = (mm * sp).astype(out_ref.dtype)

    cext[pl.ds(0, POOL_HALO), :] = chalo[...]
    coff = 0
    for width in (768, 512):
        w = next_slab()
        r = jnp.dot(h_ref[...], w[...], preferred_element_type=jnp.float32)
        cext[pl.ds(POOL_HALO, T), pl.ds(coff, width)] = r
        coff += width
    chalo[...] = cext[pl.ds(T, POOL_HALO), :]

    pos = (i % tiles_per_seq) * T + lax.broadcasted_iota(jnp.int32, (T, LANES), 0) + 1
    pos = pos.astype(jnp.float32)
    for b in range(W_C // LANES):
        lcols = pl.ds(b * LANES, LANES)
        lo_g = (b * LANES) // C_GROUP
        hi_g = (b * LANES + LANES - 1) // C_GROUP
        need = POOL_WINDOWS[hi_g]

        def shifted(s):
            return cext[pl.ds(POOL_HALO - s, T), lcols]

        cur = shifted(0)
        sums = {1: cur}
        acc_w = 1
        run = cur
        while acc_w < need:
            prev = sum(shifted(acc_w + q) for q in range(acc_w)) if acc_w > 1 else shifted(1)
            run = run + prev
            acc_w *= 2
            sums[acc_w] = run

        def mean_for(gi):
            wdw = POOL_WINDOWS[gi]
            return sums[wdw] / jnp.minimum(pos, float(wdw))

        if lo_g == hi_g:
            mean = mean_for(lo_g)
        else:
            lane = lax.broadcasted_iota(jnp.int32, (T, LANES), 1) + b * LANES
            mean = jnp.where(lane < hi_g * C_GROUP, mean_for(lo_g), mean_for(hi_g))
        dbf[:, lcols] = (mean - cur).astype(jnp.bfloat16)

    for j in range(2):
        cols = pl.ds(j * C_HALF, C_HALF)
        w = next_slab()
        yc[:, cols] = jnp.dot(dbf[:, cols], w[...], preferred_element_type=jnp.float32)

    coff = 0
    for width in (768, 512):
        cols = pl.ds(coff, width)
        w = next_slab()
        r = jnp.dot(h_ref[...], w[...], preferred_element_type=jnp.float32)
        out_ref[:, pl.ds(W_A + W_B + coff, width)] = (
            yc[:, cols] * cscale_ref[:, cols] * _silu(r)).astype(out_ref.dtype)
        coff += width

    assert state["k"] == len(SLABS)


def _mixer(h, win, apw, cw, convw, convb, alng, alnb, blng, blnb, ws, bbias, cscale, *, seq):
    n, d = h.shape
    T = TOKEN_TILE
    assert n % T == 0 and seq % T == 0 and T % CHUNK == 0 and d == D_MODEL
    full = lambda a: pl.BlockSpec(a.shape, lambda i: (0,) * a.ndim)
    hbm = pl.BlockSpec(memory_space=pl.ANY)
    return pl.pallas_call(
        functools.partial(_mixer_kernel, tiles_per_seq=seq // T),
        grid=(n // T,),
        in_specs=[pl.BlockSpec((T, d), lambda i: (i, 0)), hbm, hbm, hbm,
                  full(convw), full(convb), full(alng), full(alnb), full(blng), full(blnb),
                  full(ws), full(bbias), full(cscale)],
        out_specs=pl.BlockSpec((T, d), lambda i: (i, 0)),
        out_shape=jax.ShapeDtypeStruct((n, d), jnp.bfloat16),
        scratch_shapes=[
            pltpu.VMEM((2, D_MODEL, SLAB_W), jnp.bfloat16),
            pltpu.SemaphoreType.DMA((2,)),
            pltpu.VMEM((CONV_HALO, W_A), jnp.float32),
            pltpu.VMEM((POOL_HALO, W_C), jnp.float32),
            pltpu.VMEM((CONV_HALO + T, A_BLOCK), jnp.float32),
            pltpu.VMEM((SUBLANES - 1, CONV_HALO + T, A_BLOCK), jnp.float32),
            pltpu.VMEM((T, W_A), jnp.float32),
            pltpu.VMEM((T, W_A), jnp.float32),
            pltpu.VMEM((T, W_A), jnp.bfloat16),
            pltpu.VMEM((T, W_B), jnp.float32),
            pltpu.VMEM((POOL_HALO + T, W_C), jnp.float32),
            pltpu.VMEM((T, W_B), jnp.bfloat16),
            pltpu.VMEM((T, W_C), jnp.float32),
        ],
        compiler_params=pltpu.CompilerParams(
            dimension_semantics=("arbitrary",), vmem_limit_bytes=VMEM_LIMIT_BYTES),
        name="mixer",
    )(h, win, apw, cw, convw, convb, alng, alnb, blng, blnb, ws, bbias, cscale)


def _rms(xx, g):
    r = lax.rsqrt(jnp.mean(xx * xx, axis=-1, keepdims=True) + EPS)
    return xx * r * g


def _outproj_mid_kernel(c_ref, w_ref, x_ref, g_ref, xnew_ref, h_ref):
    j = pl.program_id(1)
    tn = w_ref.shape[1]
    y = jnp.dot(c_ref[...], w_ref[...], preferred_element_type=jnp.float32) + x_ref[...]
    xnew_ref[:, pl.ds(pl.multiple_of(j * tn, tn), tn)] = y

    @pl.when(j == pl.num_programs(1) - 1)
    def _():
        h_ref[...] = _rms(xnew_ref[...], g_ref[...]).astype(h_ref.dtype)


def _outproj_last_kernel(c_ref, w_ref, x_ref, g_ref, out_ref):
    j = pl.program_id(1)
    tn = w_ref.shape[1]
    y = jnp.dot(c_ref[...], w_ref[...], preferred_element_type=jnp.float32) + x_ref[...]
    out_ref[:, pl.ds(pl.multiple_of(j * tn, tn), tn)] = y

    @pl.when(j == pl.num_programs(1) - 1)
    def _():
        out_ref[...] = _rms(out_ref[...], g_ref[...])


def _outproj(c, w, x, g, *, last):
    n, d = x.shape
    tm, tn = OUT_TM, OUT_TN
    assert n % tm == 0 and d % tn == 0
    in_specs = [pl.BlockSpec((tm, d), lambda i, j: (i, 0)),
                pl.BlockSpec((d, tn), lambda i, j: (0, j)),
                pl.BlockSpec((tm, tn), lambda i, j: (i, j)),
                pl.BlockSpec((1, d), lambda i, j: (0, 0))]
    row_block = pl.BlockSpec((tm, d), lambda i, j: (i, 0))
    params = pltpu.CompilerParams(
        dimension_semantics=("arbitrary", "arbitrary"), vmem_limit_bytes=VMEM_LIMIT_BYTES)
    if last:
        return pl.pallas_call(
            _outproj_last_kernel, grid=(n // tm, d // tn), in_specs=in_specs,
            out_specs=row_block, out_shape=jax.ShapeDtypeStruct((n, d), jnp.float32),
            compiler_params=params, name="outproj_last")(c, w, x, g)
    return pl.pallas_call(
        _outproj_mid_kernel, grid=(n // tm, d // tn), in_specs=in_specs,
        out_specs=(row_block, row_block),
        out_shape=(jax.ShapeDtypeStruct((n, d), jnp.float32),
                   jax.ShapeDtypeStruct((n, d), jnp.bfloat16)),
        compiler_params=params, name="outproj_mid")(c, w, x, g)


def _rmsnorm_kernel(x_ref, g_ref, h_ref):
    h_ref[...] = _rms(x_ref[...], g_ref[...]).astype(h_ref.dtype)


def _rmsnorm_bf16(x, g):
    n, d = x.shape
    assert n % NORM_TM == 0
    return pl.pallas_call(
        _rmsnorm_kernel, grid=(n // NORM_TM,),
        in_specs=[pl.BlockSpec((NORM_TM, d), lambda i: (i, 0)),
                  pl.BlockSpec((1, d), lambda i: (0, 0))],
        out_specs=pl.BlockSpec((NORM_TM, d), lambda i: (i, 0)),
        out_shape=jax.ShapeDtypeStruct((n, d), jnp.bfloat16),
        compiler_params=pltpu.CompilerParams(dimension_semantics=("arbitrary",)),
        name="rmsnorm_in")(x, g)


def _pack_w_in(w):
    d = w.shape[0]
    nb = W_A // A_BLOCK
    a = w[:, :3 * W_A].reshape(d, 3, nb, A_BLOCK).transpose(0, 2, 1, 3).reshape(d, 3 * W_A)
    o = 3 * W_A
    bu, bv, bg = w[:, o:o + W_B], w[:, o + W_B:o + 2 * W_B], w[:, o + 2 * W_B:o + 3 * W_B]
    pair = 2 * HEAD_DIM
    bug = jnp.stack([bu.reshape(d, W_B // pair, pair), bg.reshape(d, W_B // pair, pair)],
                    axis=2).reshape(d, 2 * W_B)
    rest = w[:, o + 3 * W_B:]
    return jnp.concatenate([a, bv, bug, rest], axis=1).astype(jnp.bfloat16)


def _pack_c_w(cw):
    z = jnp.zeros((C_GROUP, C_GROUP), cw.dtype)
    half = lambda a, b: jnp.concatenate(
        [jnp.concatenate([a, z], axis=1), jnp.concatenate([z, b], axis=1)], axis=0)
    return jnp.concatenate([half(cw[0], cw[1]), half(cw[2], cw[3])], axis=1).astype(jnp.bfloat16)


def kernel(x, norm_g, w_in, conv_w, conv_b, a_ln_g, a_ln_b, a_pw, b_ln_g, b_ln_b, b_ws, b_bias,
           c_w, c_scale, w_out, final_g):
    bsz, seq, d = x.shape
    depth = w_in.shape[0]
    xf = x.reshape(bsz * seq, d)
    row = lambda v: v.reshape(1, -1)
    h = _rmsnorm_bf16(xf, row(norm_g[0]))
    for l in range(depth):
        convw = jnp.pad(conv_w[l], ((0, CONV_HALO - CONV_WIDTH), (0, 0)))
        cat = _mixer(h, _pack_w_in(w_in[l]), a_pw[l].astype(jnp.bfloat16), _pack_c_w(c_w[l]),
                     convw, row(conv_b[l]), row(a_ln_g[l]), row(a_ln_b[l]),
                     row(b_ln_g[l]), row(b_ln_b[l]), b_ws[l], b_bias[l][:, :, None],
                     row(c_scale[l]), seq=seq)
        wo = w_out[l].astype(jnp.bfloat16)
        if l + 1 < depth:
            xf, h = _outproj(cat, wo, xf, row(norm_g[l + 1]), last=False)
        else:
            xf = _outproj(cat, wo, xf, row(final_g), last=True)
    return xf.reshape(bsz, seq, d)
```

```python
import functools

import jax
import jax.numpy as jnp
from jax import lax
from jax.experimental import pallas as pl
from jax.experimental.pallas import tpu as pltpu

D_MODEL = 4096
HEAD_DIM = 128
W_A = 1536
W_B = 1280
W_C = 1280
H_B = W_B // HEAD_DIM
CONV_WIDTH = 31
CHUNK = 128
POOL_WINDOWS = (2, 4, 8, 16)
C_GROUP = W_C // len(POOL_WINDOWS)
EPS = 1e-6

COL_A_VAL = 0
COL_A_GLU = W_A
COL_A_GATE = 2 * W_A
COL_B_U = 3 * W_A
COL_B_V = COL_B_U + W_B
COL_B_GATE = COL_B_V + W_B
COL_C_IN = COL_B_GATE + W_B
COL_C_GATE = COL_C_IN + W_C

LANES = 128
SUBLANES = 8
VMEM_LIMIT_BYTES = 56 * 1024 * 1024

TOKEN_TILE = 512
SLAB_W = 512
A_BLOCK = SLAB_W // 2
PAIR = 2 * HEAD_DIM
C_HALF = W_C // 2
CONV_HALO = 32
POOL_HALO = 16
CONV_ROWS = 64
LN_ROWS = 64
OUT_TM = 512
OUT_TN = 512
NORM_TM = 256


def _task_list():
    t = []
    for j in range(W_A // A_BLOCK):
        t.append(("avg", j, SLAB_W, [("a_val", j * A_BLOCK, A_BLOCK, 0),
                                     ("a_glu", j * A_BLOCK, A_BLOCK, A_BLOCK)]))
    for q in range(W_A // SLAB_W):
        t.append(("agate", q, SLAB_W, [("a_gate", q * SLAB_W, SLAB_W, 0)]))
    for s in range((W_B + W_C) // SLAB_W):
        lo, hi, parts = s * SLAB_W, (s + 1) * SLAB_W, []
        if lo < W_B:
            parts.append(("b_v", lo, min(hi, W_B) - lo, 0))
        if hi > W_B:
            start = max(lo, W_B)
            parts.append(("c_in", start - W_B, hi - start, start - lo))
        t.append(("bvc", s, SLAB_W, parts))
    for j in range(W_A // SLAB_W):
        t.append(("pw", j, SLAB_W, []))
    t.append(("cw", 0, 0, []))
    off = 0
    for j, w in enumerate((SLAB_W, SLAB_W, W_C - 2 * SLAB_W)):
        t.append(("cg", j, w, [("c_gate", off, w, 0)]))
        off += w
    for p in range(H_B // 2):
        t.append(("bug", p, 2 * PAIR, [("b_u", p * PAIR, PAIR, 0), ("b_gate", p * PAIR, PAIR, PAIR)]))
    assert len(t) % 2 == 0
    assert sum(1 for x in t if x[3]) % 2 == 0
    return t


TASKS = _task_list()
GROUP_COL = {"a_val": COL_A_VAL, "a_glu": COL_A_GLU, "a_gate": COL_A_GATE, "b_u": COL_B_U,
             "b_v": COL_B_V, "b_gate": COL_B_GATE, "c_in": COL_C_IN, "c_gate": COL_C_GATE}
SLAB_TASK = [k for k, x in enumerate(TASKS) if x[3]]
TASK_SLAB = {k: s for s, k in enumerate(SLAB_TASK)}


def _sigmoid(x):
    return 1.0 / (1.0 + jnp.exp(-x))


def _silu(x):
    return x * _sigmoid(x)


def _gelu(x):
    return jax.nn.gelu(x)


def _mixer_kernel(h_ref, win_hbm, apw_ref, cw_ref, convw_ref, convb_ref, alng_ref, alnb_ref,
                  blng_ref, blnb_ref, ws_ref, bbias_ref, cscale_ref,
                  out_ref,
                  wbuf, sem, rbuf0, rbuf1, cwb, ghalo, chalo, gext, gsh, ca, sgate, abf, gv, vnb, cext, dbf,
                  *, tiles_per_seq):
    yc = gv
    rbufs = (rbuf0, rbuf1)
    i = pl.program_id(0)
    n_tiles = pl.num_programs(0)
    T = h_ref.shape[0]
    n_tasks, n_slabs = len(TASKS), len(SLAB_TASK)

    def slab_copies(s):
        slot = s % 2
        return [pltpu.make_async_copy(
            win_hbm.at[:, pl.ds(GROUP_COL[group] + col, width)],
            wbuf.at[slot, :, pl.ds(dst, width)],
            sem.at[slot]) for group, col, width, dst in TASKS[SLAB_TASK[s]][3]]

    def start_slab(s):
        for c in slab_copies(s):
            c.start()

    def wait_slab(s):
        for c in slab_copies(s):
            c.wait()

    def issue_dot(k):
        kind, j, width, _ = TASKS[k]
        f32 = jnp.float32
        if kind == "cw":
            for q in range(2):
                cols = pl.ds(q * C_HALF, C_HALF)
                yc[:, cols] = jnp.dot(dbf[:, cols], cw_ref[:, cols], preferred_element_type=f32)
        elif kind == "pw":
            rbufs[k % 2][...] = jnp.dot(abf[...], apw_ref[:, pl.ds(j * SLAB_W, SLAB_W)],
                                        preferred_element_type=f32)
        else:
            w = wbuf[TASK_SLAB[k] % 2, :, pl.ds(0, width)]
            rbufs[k % 2][:, pl.ds(0, width)] = jnp.dot(h_ref[...], w, preferred_element_type=f32)

    def epi_avg(k, j):
        r = rbufs[k % 2]
        cols = pl.ds(j * A_BLOCK, A_BLOCK)
        g = r[:, pl.ds(0, A_BLOCK)] * _sigmoid(r[:, pl.ds(A_BLOCK, A_BLOCK)])
        gext[pl.ds(0, CONV_HALO), :] = ghalo[:, cols]
        gext[pl.ds(CONV_HALO, T), :] = g
        ghalo[:, cols] = g[T - CONV_HALO:, :]
        for b in range(1, SUBLANES):
            gsh[b - 1, pl.ds(0, T + CONV_HALO - SUBLANES), :] = gext[pl.ds(b, T + CONV_HALO - SUBLANES), :]
        groups = CONV_ROWS // SUBLANES
        bias = convb_ref[:, cols]
        for c in range(T // CONV_ROWS):
            acc = jnp.broadcast_to(bias, (SUBLANES, A_BLOCK))[None]
            for tap in range(CONV_WIDTH):
                a, b = divmod(CONV_HALO - (CONV_WIDTH - 1) + tap, SUBLANES)
                src = gext if b == 0 else gsh.at[b - 1]
                x = src[pl.ds(c * CONV_ROWS + a * SUBLANES, CONV_ROWS), :]
                wt = cwb[pl.ds(tap * SUBLANES, SUBLANES), cols]
                acc = acc + x.reshape(groups, SUBLANES, A_BLOCK) * wt[None]
            ca[pl.ds(c * CONV_ROWS, CONV_ROWS), cols] = acc.reshape(CONV_ROWS, A_BLOCK)

    def epi_agate(k, q):
        sgate[:, pl.ds(q * SLAB_W, SLAB_W)] = _silu(rbufs[k % 2][...])

    def layer_norm_rows(src, c, gamma, beta):
        xx = src[pl.ds(c * LN_ROWS, LN_ROWS), :]
        mu = jnp.mean(xx, axis=-1, keepdims=True)
        xc = xx - mu
        var = jnp.mean(xc * xc, axis=-1, keepdims=True)
        return xc * lax.rsqrt(var + EPS) * gamma[...] + beta[...]

    def ln_a():
        for c in range(T // LN_ROWS):
            y = layer_norm_rows(ca, c, alng_ref, alnb_ref)
            abf[pl.ds(c * LN_ROWS, LN_ROWS), :] = _silu(y).astype(jnp.bfloat16)

    def ln_b():
        for c in range(T // LN_ROWS):
            y = layer_norm_rows(gv, c, blng_ref, blnb_ref)
            vnb[pl.ds(c * LN_ROWS, LN_ROWS), :] = y.astype(jnp.bfloat16)

    def epi_pw(k, j):
        cols = pl.ds(j * SLAB_W, SLAB_W)
        out_ref[:, cols] = (rbufs[k % 2][...] * sgate[:, cols]).astype(out_ref.dtype)

    def epi_bvc(k, s):
        if s == 0:
            cext[pl.ds(0, POOL_HALO), :] = chalo[...]
        for group, col, width, at in TASKS[k][3]:
            r = rbufs[k % 2][:, pl.ds(at, width)]
            if group == "b_v":
                gv[:, pl.ds(col, width)] = _gelu(r)
            else:
                cext[pl.ds(POOL_HALO, T), pl.ds(col, width)] = r

    def epi_bug(k, p):
        r = rbufs[k % 2]
        m = _gelu(r[:, pl.ds(0, PAIR)]) * _silu(r[:, pl.ds(PAIR, PAIR)])
        row_id = lax.broadcasted_iota(jnp.int32, (CHUNK, CHUNK), 0)
        col_id = lax.broadcasted_iota(jnp.int32, (CHUNK, CHUNK), 1)
        for hh in range(2):
            head = 2 * p + hh
            wsm = jnp.where(col_id <= row_id, ws_ref[head], 0.0).astype(jnp.bfloat16)
            bcol = bbias_ref[head]
            ccols = pl.ds(head * HEAD_DIM, HEAD_DIM)
            for n in range(T // CHUNK):
                rows = pl.ds(n * CHUNK, CHUNK)
                sp = jnp.dot(wsm, vnb[rows, ccols], preferred_element_type=jnp.float32) + bcol
                mm = m[n * CHUNK:(n + 1) * CHUNK, hh * HEAD_DIM:(hh + 1) * HEAD_DIM]
                out_ref[rows, pl.ds(W_A + head * HEAD_DIM, HEAD_DIM)] = (mm * sp).astype(out_ref.dtype)

    def pool():
        chalo[...] = cext[pl.ds(T, POOL_HALO), :]
        pos = (i % tiles_per_seq) * T + lax.broadcasted_iota(jnp.int32, (T, LANES), 0) + 1
        pos = pos.astype(jnp.float32)
        for b in range(W_C // LANES):
            lcols = pl.ds(b * LANES, LANES)
            lo_g = (b * LANES) // C_GROUP
            hi_g = (b * LANES + LANES - 1) // C_GROUP
            cur = cext[pl.ds(POOL_HALO, T), lcols]
            sums = {1: cur}
            run, have = cur, 1
            while have < POOL_WINDOWS[hi_g]:
                for s in range(have, 2 * have):
                    run = run + cext[pl.ds(POOL_HALO - s, T), lcols]
                have *= 2
                sums[have] = run

            def mean_for(gi):
                wdw = POOL_WINDOWS[gi]
                return sums[wdw] / jnp.minimum(pos, float(wdw))

            if lo_g == hi_g:
                mean = mean_for(lo_g)
            else:
                lane = lax.broadcasted_iota(jnp.int32, (T, LANES), 1) + b * LANES
                mean = jnp.where(lane < hi_g * C_GROUP, mean_for(lo_g), mean_for(hi_g))
            dbf[:, lcols] = (mean - cur).astype(jnp.bfloat16)

    def epi_cg(k, j):
        _, off, w, _ = TASKS[k][3][0]
        cols = pl.ds(off, w)
        out_ref[:, pl.ds(W_A + W_B + off, w)] = (
            yc[:, cols] * cscale_ref[:, cols] * _silu(rbufs[k % 2][:, pl.ds(0, w)])).astype(out_ref.dtype)

    epilogue = {"avg": epi_avg, "agate": epi_agate, "bvc": epi_bvc, "pw": epi_pw,
                "cw": lambda k, j: None, "cg": epi_cg, "bug": epi_bug}
    last_bv = (W_B - 1) // SLAB_W
    last_cin = (W_B + W_C) // SLAB_W - 1
    after = {("agate", 0): ln_a, ("bvc", last_bv): ln_b, ("bvc", last_cin): pool}

    @pl.when(i == 0)
    def _():
        start_slab(0)
        for tap in range(CONV_WIDTH):
            cwb[pl.ds(tap * SUBLANES, SUBLANES), :] = jnp.broadcast_to(
                convw_ref[pl.ds(tap, 1), :], (SUBLANES, W_A))

    @pl.when((i % tiles_per_seq) == 0)
    def _():
        ghalo[...] = jnp.zeros_like(ghalo)
        chalo[...] = jnp.zeros_like(chalo)

    def fetch_for(k):
        s = TASK_SLAB.get(k)
        if s is None:
            return
        if s + 1 < n_slabs:
            start_slab(s + 1)
        else:
            @pl.when(i + 1 < n_tiles)
            def _():
                start_slab(0)
        wait_slab(s)

    fetch_for(0)
    issue_dot(0)

    for k in range(n_tasks):
        if k + 1 < n_tasks:
            fetch_for(k + 1)
            issue_dot(k + 1)
        kind, j = TASKS[k][0], TASKS[k][1]
        epilogue[kind](k, j)
        extra = after.get((kind, j))
        if extra is not None:
            extra()


def _mixer(h, win, apw, cw, convw, convb, alng, alnb, blng, blnb, ws, bbias, cscale, *, seq):
    n, d = h.shape
    T = TOKEN_TILE
    assert n % T == 0 and seq % T == 0 and T % CHUNK == 0 and d == D_MODEL
    assert W_B == W_C
    full = lambda a: pl.BlockSpec(a.shape, lambda i: (0,) * a.ndim, pipeline_mode=pl.Buffered(1))
    return pl.pallas_call(
        functools.partial(_mixer_kernel, tiles_per_seq=seq // T),
        grid=(n // T,),
        in_specs=[pl.BlockSpec((T, d), lambda i: (i, 0)), pl.BlockSpec(memory_space=pl.ANY),
                  full(apw), full(cw),
                  full(convw), full(convb), full(alng), full(alnb), full(blng), full(blnb),
                  full(ws), full(bbias), full(cscale)],
        out_specs=pl.BlockSpec((T, d), lambda i: (i, 0)),
        out_shape=jax.ShapeDtypeStruct((n, d), jnp.bfloat16),
        scratch_shapes=[
            pltpu.VMEM((2, D_MODEL, SLAB_W), jnp.bfloat16),
            pltpu.SemaphoreType.DMA((2,)),
            pltpu.VMEM((T, SLAB_W), jnp.float32),
            pltpu.VMEM((T, SLAB_W), jnp.float32),
            pltpu.VMEM((CONV_WIDTH * SUBLANES, W_A), jnp.float32),
            pltpu.VMEM((CONV_HALO, W_A), jnp.float32),
            pltpu.VMEM((POOL_HALO, W_C), jnp.float32),
            pltpu.VMEM((CONV_HALO + T, A_BLOCK), jnp.float32),
            pltpu.VMEM((SUBLANES - 1, CONV_HALO + T, A_BLOCK), jnp.float32),
            pltpu.VMEM((T, W_A), jnp.float32),
            pltpu.VMEM((T, W_A), jnp.float32),
            pltpu.VMEM((T, W_A), jnp.bfloat16),
            pltpu.VMEM((T, W_B), jnp.float32),
            pltpu.VMEM((T, W_B), jnp.bfloat16),
            pltpu.VMEM((POOL_HALO + T, W_C), jnp.float32),
            pltpu.VMEM((T, W_C), jnp.bfloat16),
        ],
        compiler_params=pltpu.CompilerParams(
            dimension_semantics=("arbitrary",), vmem_limit_bytes=VMEM_LIMIT_BYTES),
        name="mixer",
    )(h, win, apw, cw, convw, convb, alng, alnb, blng, blnb, ws, bbias, cscale)


def _rms(xx, g):
    r = lax.rsqrt(jnp.mean(xx * xx, axis=-1, keepdims=True) + EPS)
    return xx * r * g


def _outproj_mid_kernel(c_ref, w_ref, x_ref, g_ref, xnew_ref, h_ref):
    j = pl.program_id(1)
    tn = w_ref.shape[1]
    y = jnp.dot(c_ref[...], w_ref[...], preferred_element_type=jnp.float32) + x_ref[...]
    xnew_ref[:, pl.ds(pl.multiple_of(j * tn, tn), tn)] = y

    @pl.when(j == pl.num_programs(1) - 1)
    def _():
        h_ref[...] = _rms(xnew_ref[...], g_ref[...]).astype(h_ref.dtype)


def _outproj_last_kernel(c_ref, w_ref, x_ref, g_ref, out_ref):
    j = pl.program_id(1)
    tn = w_ref.shape[1]
    y = jnp.dot(c_ref[...], w_ref[...], preferred_element_type=jnp.float32) + x_ref[...]
    out_ref[:, pl.ds(pl.multiple_of(j * tn, tn), tn)] = y

    @pl.when(j == pl.num_programs(1) - 1)
    def _():
        out_ref[...] = _rms(out_ref[...], g_ref[...])


def _outproj(c, w, x, g, *, last):
    n, d = x.shape
    tm, tn = OUT_TM, OUT_TN
    assert n % tm == 0 and d % tn == 0
    in_specs = [pl.BlockSpec((tm, d), lambda i, j: (i, 0)),
                pl.BlockSpec((d, tn), lambda i, j: (0, j)),
                pl.BlockSpec((tm, tn), lambda i, j: (i, j)),
                pl.BlockSpec((1, d), lambda i, j: (0, 0))]
    row_block = pl.BlockSpec((tm, d), lambda i, j: (i, 0))
    params = pltpu.CompilerParams(
        dimension_semantics=("arbitrary", "arbitrary"), vmem_limit_bytes=VMEM_LIMIT_BYTES)
    if last:
        return pl.pallas_call(
            _outproj_last_kernel, grid=(n // tm, d // tn), in_specs=in_specs,
            out_specs=row_block, out_shape=jax.ShapeDtypeStruct((n, d), jnp.float32),
            compiler_params=params, name="outproj_last")(c, w, x, g)
    return pl.pallas_call(
        _outproj_mid_kernel, grid=(n // tm, d // tn), in_specs=in_specs,
        out_specs=(row_block, row_block),
        out_shape=(jax.ShapeDtypeStruct((n, d), jnp.float32),
                   jax.ShapeDtypeStruct((n, d), jnp.bfloat16)),
        compiler_params=params, name="outproj_mid")(c, w, x, g)


def _rmsnorm_kernel(x_ref, g_ref, h_ref):
    h_ref[...] = _rms(x_ref[...], g_ref[...]).astype(h_ref.dtype)


def _rmsnorm_bf16(x, g):
    n, d = x.shape
    assert n % NORM_TM == 0
    return pl.pallas_call(
        _rmsnorm_kernel, grid=(n // NORM_TM,),
        in_specs=[pl.BlockSpec((NORM_TM, d), lambda i: (i, 0)),
                  pl.BlockSpec((1, d), lambda i: (0, 0))],
        out_specs=pl.BlockSpec((NORM_TM, d), lambda i: (i, 0)),
        out_shape=jax.ShapeDtypeStruct((n, d), jnp.bfloat16),
        compiler_params=pltpu.CompilerParams(dimension_semantics=("arbitrary",)),
        name="rmsnorm_in")(x, g)


def _pack_c_w(cw):
    z = jnp.zeros((C_GROUP, C_GROUP), cw.dtype)
    half = lambda a, b: jnp.concatenate(
        [jnp.concatenate([a, z], axis=1), jnp.concatenate([z, b], axis=1)], axis=0)
    return jnp.concatenate([half(cw[0], cw[1]), half(cw[2], cw[3])], axis=1).astype(jnp.bfloat16)


def kernel(x, norm_g, w_in, conv_w, conv_b, a_ln_g, a_ln_b, a_pw, b_ln_g, b_ln_b, b_ws, b_bias,
           c_w, c_scale, w_out, final_g):
    bsz, seq, d = x.shape
    depth = w_in.shape[0]
    xf = x.reshape(bsz * seq, d)
    row = lambda v: v.reshape(1, -1)
    h = _rmsnorm_bf16(xf, row(norm_g[0]))
    for l in range(depth):
        cat = _mixer(h, w_in[l].astype(jnp.bfloat16), a_pw[l].astype(jnp.bfloat16), _pack_c_w(c_w[l]),
                     conv_w[l], row(conv_b[l]), row(a_ln_g[l]), row(a_ln_b[l]),
                     row(b_ln_g[l]), row(b_ln_b[l]), b_ws[l], b_bias[l][:, :, None],
                     row(c_scale[l]), seq=seq)
        wo = w_out[l].astype(jnp.bfloat16)
        if l + 1 < depth:
            xf, h = _outproj(cat, wo, xf, row(norm_g[l + 1]), last=False)
        else:
            xf = _outproj(cat, wo, xf, row(final_g), last=True)
    return xf.reshape(bsz, seq, d)
```

```python
import functools

import jax
import jax.numpy as jnp
from jax import lax
from jax.experimental import pallas as pl
from jax.experimental.pallas import tpu as pltpu

D_MODEL = 4096
HEAD_DIM = 128
W_A = 1536
W_B = 1280
W_C = 1280
H_B = W_B // HEAD_DIM
CONV_WIDTH = 31
CHUNK = 128
POOL_WINDOWS = (2, 4, 8, 16)
C_GROUP = W_C // len(POOL_WINDOWS)
EPS = 1e-6

COL_A_VAL = 0
COL_A_GLU = W_A
COL_A_GATE = 2 * W_A
COL_B_U = 3 * W_A
COL_B_V = COL_B_U + W_B
COL_B_GATE = COL_B_V + W_B
COL_C_IN = COL_B_GATE + W_B
COL_C_GATE = COL_C_IN + W_C

LANES = 128
SUBLANES = 8
VMEM_LIMIT_BYTES = 56 * 1024 * 1024

TOKEN_TILE = 512
SLAB_W = 512
A_BLOCK = SLAB_W // 2
PAIR = 2 * HEAD_DIM
C_HALF = W_C // 2
CONV_HALO = 32
POOL_HALO = 16
GLU_ROWS = 128
LN_ROWS = 64
OUT_TM = 512
OUT_TN = 512
NORM_TM = 256


def _task_list():
    t = []
    for j in range(W_A // A_BLOCK):
        t.append(("avg", j, SLAB_W, [("a_val", j * A_BLOCK, A_BLOCK, 0),
                                     ("a_glu", j * A_BLOCK, A_BLOCK, A_BLOCK)]))
    for q in range(W_A // SLAB_W):
        t.append(("agate", q, SLAB_W, [("a_gate", q * SLAB_W, SLAB_W, 0)]))
    for s in range((W_B + W_C) // SLAB_W):
        lo, hi, parts = s * SLAB_W, (s + 1) * SLAB_W, []
        if lo < W_B:
            parts.append(("b_v", lo, min(hi, W_B) - lo, 0))
        if hi > W_B:
            start = max(lo, W_B)
            parts.append(("c_in", start - W_B, hi - start, start - lo))
        t.append(("bvc", s, SLAB_W, parts))
    for j in range(W_A // SLAB_W):
        t.append(("pw", j, SLAB_W, []))
    t.append(("cw", 0, 0, []))
    off = 0
    for j, w in enumerate((SLAB_W, SLAB_W, W_C - 2 * SLAB_W)):
        t.append(("cg", j, w, [("c_gate", off, w, 0)]))
        off += w
    for p in range(H_B // 2):
        t.append(("bug", p, 2 * PAIR, [("b_u", p * PAIR, PAIR, 0), ("b_gate", p * PAIR, PAIR, PAIR)]))
    assert len(t) % 2 == 0
    assert sum(1 for x in t if x[3]) % 2 == 0
    return t


TASKS = _task_list()
GROUP_COL = {"a_val": COL_A_VAL, "a_glu": COL_A_GLU, "a_gate": COL_A_GATE, "b_u": COL_B_U,
             "b_v": COL_B_V, "b_gate": COL_B_GATE, "c_in": COL_C_IN, "c_gate": COL_C_GATE}
SLAB_TASK = [k for k, x in enumerate(TASKS) if x[3]]
TASK_SLAB = {k: s for s, k in enumerate(SLAB_TASK)}
LAST_BV = (W_B - 1) // SLAB_W
LAST_CIN = (W_B + W_C) // SLAB_W - 1


def _sigmoid(x):
    return 1.0 / (1.0 + jnp.exp(-x))


def _silu(x):
    return x * _sigmoid(x)


def _gelu(x):
    return jax.nn.gelu(x)


def _mixer_kernel(h_ref, win_hbm, apw_ref, cw_ref, convw_ref, convb_ref, alng_ref, alnb_ref,
                  blng_ref, blnb_ref, ws_ref, bbias_ref, cscale_ref,
                  out_ref,
                  wbuf, sem, rbuf0, rbuf1, cwb, ghalo, chalo, gext, gsh, ca, sgate, abf, gv, vnb, cext, dbf,
                  *, layer, tiles_per_seq):
    yc = gv
    rbufs = (rbuf0, rbuf1)
    i = pl.program_id(0)
    n_tiles = pl.num_programs(0)
    T = h_ref.shape[0]
    n_tasks, n_slabs = len(TASKS), len(SLAB_TASK)
    f32, bf16 = jnp.float32, jnp.bfloat16

    masks = {}

    def after(x, prev):
        if prev is None:
            return x
        if x.shape not in masks:
            masks[x.shape] = (lax.broadcasted_iota(jnp.int32, x.shape, 0) + i) >= 0
        return jnp.where(masks[x.shape], x, prev)

    def slab_copies(s):
        slot = s % 2
        return [pltpu.make_async_copy(
            win_hbm.at[layer, :, pl.ds(GROUP_COL[group] + col, width)],
            wbuf.at[slot, :, pl.ds(dst, width)],
            sem.at[slot]) for group, col, width, dst in TASKS[SLAB_TASK[s]][3]]

    def start_slab(s):
        for c in slab_copies(s):
            c.start()

    def wait_slab(s):
        for c in slab_copies(s):
            c.wait()

    def issue_dot(k):
        kind, j, width, _ = TASKS[k]
        if kind == "cw":
            for q in range(2):
                cols = pl.ds(q * C_HALF, C_HALF)
                yc[:, cols] = jnp.dot(dbf[:, cols], cw_ref[:, cols], preferred_element_type=f32)
        elif kind == "pw":
            rbufs[k % 2][...] = jnp.dot(abf[...], apw_ref[:, pl.ds(j * SLAB_W, SLAB_W)],
                                        preferred_element_type=f32)
        else:
            w = wbuf[TASK_SLAB[k] % 2, :, pl.ds(0, width)]
            rbufs[k % 2][:, pl.ds(0, width)] = jnp.dot(h_ref[...], w, preferred_element_type=f32)

    def chained_rows(n_rows, fn):
        prev = None
        for r0 in range(0, T, n_rows):
            prev = fn(pl.ds(r0, n_rows), prev)

    def vec_avg(k, j):
        r = rbufs[k % 2]
        cols = pl.ds(j * A_BLOCK, A_BLOCK)
        gext[pl.ds(0, CONV_HALO), :] = ghalo[:, cols]
        for r0 in range(0, T, GLU_ROWS):
            rows = pl.ds(r0, GLU_ROWS)
            g = r[rows, pl.ds(0, A_BLOCK)] * _sigmoid(r[rows, pl.ds(A_BLOCK, A_BLOCK)])
            gext[pl.ds(CONV_HALO + r0, GLU_ROWS), :] = g
            if r0 + GLU_ROWS == T:
                ghalo[:, cols] = g[GLU_ROWS - CONV_HALO:, :]
        n_shift = T + CONV_HALO - SUBLANES
        for b in range(1, SUBLANES):
            gsh[b - 1, pl.ds(0, n_shift), :] = gext[pl.ds(b, n_shift), :]
        bias = jnp.broadcast_to(convb_ref[:, cols], (SUBLANES, A_BLOCK))
        acc = None
        for r0 in range(0, T, SUBLANES):
            acc = after(bias, acc)
            for tap in range(CONV_WIDTH):
                a, b = divmod(CONV_HALO - (CONV_WIDTH - 1) + tap, SUBLANES)
                src = gext if b == 0 else gsh.at[b - 1]
                acc = acc + (src[pl.ds(r0 + a * SUBLANES, SUBLANES), :]
                             * cwb[pl.ds(tap * SUBLANES, SUBLANES), cols])
            ca[pl.ds(r0, SUBLANES), cols] = acc

    def vec_agate(k, q):
        def group(rows, prev):
            y = _silu(after(rbufs[k % 2][rows, :], prev))
            sgate[rows, pl.ds(q * SLAB_W, SLAB_W)] = y
            return y
        chained_rows(SUBLANES, group)

    def layer_norm(xx, gamma, beta):
        mu = jnp.mean(xx, axis=-1, keepdims=True)
        xc = xx - mu
        var = jnp.mean(xc * xc, axis=-1, keepdims=True)
        return xc * lax.rsqrt(var + EPS) * gamma[...] + beta[...]

    def vec_ln_a():
        def group(rows, prev):
            y = _silu(layer_norm(after(ca[rows, :], prev), alng_ref, alnb_ref))
            abf[rows, :] = y.astype(bf16)
            return y
        chained_rows(LN_ROWS, group)

    def vec_ln_b():
        def group(rows, prev):
            y = layer_norm(after(gv[rows, :], prev), blng_ref, blnb_ref)
            vnb[rows, :] = y.astype(bf16)
            return y
        chained_rows(LN_ROWS, group)

    def vec_pw(k, j):
        cols = pl.ds(j * SLAB_W, SLAB_W)
        out_ref[:, cols] = (rbufs[k % 2][...] * sgate[:, cols]).astype(out_ref.dtype)

    def vec_bvc(k, s):
        for group_name, col, width, at in TASKS[k][3]:
            if group_name == "b_v":
                def group(rows, prev, col=col, width=width, at=at):
                    y = _gelu(after(rbufs[k % 2][rows, pl.ds(at, width)], prev))
                    gv[rows, pl.ds(col, width)] = y
                    return y
                chained_rows(SUBLANES, group)
            else:
                if col == 0:
                    cext[pl.ds(0, POOL_HALO), :] = chalo[...]
                cext[pl.ds(POOL_HALO, T), pl.ds(col, width)] = rbufs[k % 2][:, pl.ds(at, width)]

    def vec_bug(k, p):
        r = rbufs[k % 2]
        row_id = lax.broadcasted_iota(jnp.int32, (CHUNK, CHUNK), 0)
        col_id = lax.broadcasted_iota(jnp.int32, (CHUNK, CHUNK), 1)
        prev = None
        for hh in range(2):
            head = 2 * p + hh
            wsm = jnp.where(col_id <= row_id, ws_ref[head], 0.0).astype(bf16)
            for n in range(T // CHUNK):
                rows = pl.ds(n * CHUNK, CHUNK)
                u = after(r[rows, pl.ds(hh * HEAD_DIM, HEAD_DIM)], prev)
                sp = jnp.dot(wsm, vnb[rows, pl.ds(head * HEAD_DIM, HEAD_DIM)],
                             preferred_element_type=f32) + bbias_ref[head]
                prev = _gelu(u) * _silu(r[rows, pl.ds(PAIR + hh * HEAD_DIM, HEAD_DIM)]) * sp
                out_ref[rows, pl.ds(W_A + head * HEAD_DIM, HEAD_DIM)] = prev.astype(out_ref.dtype)

    def vec_pool():
        chalo[...] = cext[pl.ds(T, POOL_HALO), :]
        prev = None
        for b in range(W_C // LANES):
            lcols = pl.ds(b * LANES, LANES)
            lo_g = (b * LANES) // C_GROUP
            hi_g = (b * LANES + LANES - 1) // C_GROUP
            for r0 in range(0, T, LN_ROWS):
                pos = ((i % tiles_per_seq) * T + r0 + 1
                       + lax.broadcasted_iota(jnp.int32, (LN_ROWS, LANES), 0)).astype(f32)
                cur = after(cext[pl.ds(POOL_HALO + r0, LN_ROWS), lcols], prev)
                sums = {1: cur}
                run, have = cur, 1
                while have < POOL_WINDOWS[hi_g]:
                    for s in range(have, 2 * have):
                        run = run + cext[pl.ds(POOL_HALO + r0 - s, LN_ROWS), lcols]
                    have *= 2
                    sums[have] = run
                mean = sums[POOL_WINDOWS[hi_g]] / jnp.minimum(pos, float(POOL_WINDOWS[hi_g]))
                if lo_g != hi_g:
                    lane = lax.broadcasted_iota(jnp.int32, (LN_ROWS, LANES), 1) + b * LANES
                    mean_lo = sums[POOL_WINDOWS[lo_g]] / jnp.minimum(pos, float(POOL_WINDOWS[lo_g]))
                    mean = jnp.where(lane < hi_g * C_GROUP, mean_lo, mean)
                prev = mean - cur
                dbf[pl.ds(r0, LN_ROWS), lcols] = prev.astype(bf16)

    def vec_cg(k, j):
        _, off, w, _ = TASKS[k][3][0]
        cols = pl.ds(off, w)

        def group(rows, prev):
            y = yc[rows, cols] * cscale_ref[:, cols] * _silu(after(rbufs[k % 2][rows, pl.ds(0, w)], prev))
            out_ref[rows, pl.ds(W_A + W_B + off, w)] = y.astype(out_ref.dtype)
            return y
        chained_rows(SUBLANES, group)

    def vector_work(k):
        kind, j = TASKS[k][0], TASKS[k][1]
        if kind == "avg":
            vec_avg(k, j)
        elif kind == "agate":
            vec_agate(k, j)
            if j == 0:
                vec_ln_a()
        elif kind == "bvc":
            vec_bvc(k, j)
            if j == LAST_BV:
                vec_ln_b()
            if j == LAST_CIN:
                vec_pool()
        elif kind == "pw":
            vec_pw(k, j)
        elif kind == "cg":
            vec_cg(k, j)
        elif kind == "bug":
            vec_bug(k, j)

    @pl.when(i == 0)
    def _():
        start_slab(0)
        for tap in range(CONV_WIDTH):
            cwb[pl.ds(tap * SUBLANES, SUBLANES), :] = jnp.broadcast_to(
                convw_ref[pl.ds(tap, 1), :], (SUBLANES, W_A))

    @pl.when((i % tiles_per_seq) == 0)
    def _():
        ghalo[...] = jnp.zeros_like(ghalo)
        chalo[...] = jnp.zeros_like(chalo)

    def fetch_for(k):
        s = TASK_SLAB.get(k)
        if s is None:
            return
        if s + 1 < n_slabs:
            start_slab(s + 1)
        else:
            @pl.when(i + 1 < n_tiles)
            def _():
                start_slab(0)
        wait_slab(s)

    fetch_for(0)
    issue_dot(0)

    for k in range(n_tasks):
        if k + 1 < n_tasks:
            fetch_for(k + 1)
            issue_dot(k + 1)
        vector_work(k)


def _mixer(h, win, apw, cw, convw, convb, alng, alnb, blng, blnb, ws, bbias, cscale, *, layer, seq):
    n, d = h.shape
    T = TOKEN_TILE
    assert n % T == 0 and seq % T == 0 and T % CHUNK == 0 and d == D_MODEL
    assert W_B == W_C

    def per_layer(a):
        return pl.BlockSpec((None,) + a.shape[1:], lambda i: (layer,) + (0,) * (a.ndim - 1),
                            pipeline_mode=pl.Buffered(1))

    return pl.pallas_call(
        functools.partial(_mixer_kernel, layer=layer, tiles_per_seq=seq // T),
        grid=(n // T,),
        in_specs=[pl.BlockSpec((T, d), lambda i: (i, 0)), pl.BlockSpec(memory_space=pl.ANY),
                  per_layer(apw), per_layer(cw),
                  per_layer(convw), per_layer(convb), per_layer(alng), per_layer(alnb),
                  per_layer(blng), per_layer(blnb), per_layer(ws), per_layer(bbias), per_layer(cscale)],
        out_specs=pl.BlockSpec((T, d), lambda i: (i, 0)),
        out_shape=jax.ShapeDtypeStruct((n, d), jnp.bfloat16),
        scratch_shapes=[
            pltpu.VMEM((2, D_MODEL, SLAB_W), jnp.bfloat16),
            pltpu.SemaphoreType.DMA((2,)),
            pltpu.VMEM((T, SLAB_W), jnp.float32),
            pltpu.VMEM((T, SLAB_W), jnp.float32),
            pltpu.VMEM((CONV_WIDTH * SUBLANES, W_A), jnp.float32),
            pltpu.VMEM((CONV_HALO, W_A), jnp.float32),
            pltpu.VMEM((POOL_HALO, W_C), jnp.float32),
            pltpu.VMEM((CONV_HALO + T, A_BLOCK), jnp.float32),
            pltpu.VMEM((SUBLANES - 1, CONV_HALO + T, A_BLOCK), jnp.float32),
            pltpu.VMEM((T, W_A), jnp.float32),
            pltpu.VMEM((T, W_A), jnp.float32),
            pltpu.VMEM((T, W_A), jnp.bfloat16),
            pltpu.VMEM((T, W_B), jnp.float32),
            pltpu.VMEM((T, W_B), jnp.bfloat16),
            pltpu.VMEM((POOL_HALO + T, W_C), jnp.float32),
            pltpu.VMEM((T, W_C), jnp.bfloat16),
        ],
        compiler_params=pltpu.CompilerParams(
            dimension_semantics=("arbitrary",), vmem_limit_bytes=VMEM_LIMIT_BYTES),
        name="mixer",
    )(h, win, apw, cw, convw, convb, alng, alnb, blng, blnb, ws, bbias, cscale)


def _rms(xx, g):
    r = lax.rsqrt(jnp.mean(xx * xx, axis=-1, keepdims=True) + EPS)
    return xx * r * g


def _outproj_mid_kernel(c_ref, w_ref, x_ref, g_ref, xnew_ref, h_ref):
    j = pl.program_id(1)
    tn = w_ref.shape[1]
    y = jnp.dot(c_ref[...], w_ref[...], preferred_element_type=jnp.float32) + x_ref[...]
    xnew_ref[:, pl.ds(pl.multiple_of(j * tn, tn), tn)] = y

    @pl.when(j == pl.num_programs(1) - 1)
    def _():
        h_ref[...] = _rms(xnew_ref[...], g_ref[...]).astype(h_ref.dtype)


def _outproj_last_kernel(c_ref, w_ref, x_ref, g_ref, out_ref):
    j = pl.program_id(1)
    tn = w_ref.shape[1]
    y = jnp.dot(c_ref[...], w_ref[...], preferred_element_type=jnp.float32) + x_ref[...]
    out_ref[:, pl.ds(pl.multiple_of(j * tn, tn), tn)] = y

    @pl.when(j == pl.num_programs(1) - 1)
    def _():
        out_ref[...] = _rms(out_ref[...], g_ref[...])


def _outproj(c, w, x, g, *, layer, last):
    n, d = x.shape
    tm, tn = OUT_TM, OUT_TN
    assert n % tm == 0 and d % tn == 0
    in_specs = [pl.BlockSpec((tm, d), lambda i, j: (i, 0)),
                pl.BlockSpec((None, d, tn), lambda i, j: (layer, 0, j)),
                pl.BlockSpec((tm, tn), lambda i, j: (i, j)),
                pl.BlockSpec((1, d), lambda i, j: (0, 0))]
    row_block = pl.BlockSpec((tm, d), lambda i, j: (i, 0))
    params = pltpu.CompilerParams(
        dimension_semantics=("arbitrary", "arbitrary"), vmem_limit_bytes=VMEM_LIMIT_BYTES)
    if last:
        return pl.pallas_call(
            _outproj_last_kernel, grid=(n // tm, d // tn), in_specs=in_specs,
            out_specs=row_block, out_shape=jax.ShapeDtypeStruct((n, d), jnp.float32),
            compiler_params=params, name="outproj_last")(c, w, x, g)
    return pl.pallas_call(
        _outproj_mid_kernel, grid=(n // tm, d // tn), in_specs=in_specs,
        out_specs=(row_block, row_block),
        out_shape=(jax.ShapeDtypeStruct((n, d), jnp.float32),
                   jax.ShapeDtypeStruct((n, d), jnp.bfloat16)),
        compiler_params=params, name="outproj_mid")(c, w, x, g)


def _rmsnorm_kernel(x_ref, g_ref, h_ref):
    h_ref[...] = _rms(x_ref[...], g_ref[...]).astype(h_ref.dtype)


def _rmsnorm_bf16(x, g):
    n, d = x.shape
    assert n % NORM_TM == 0
    return pl.pallas_call(
        _rmsnorm_kernel, grid=(n // NORM_TM,),
        in_specs=[pl.BlockSpec((NORM_TM, d), lambda i: (i, 0)),
                  pl.BlockSpec((1, d), lambda i: (0, 0))],
        out_specs=pl.BlockSpec((NORM_TM, d), lambda i: (i, 0)),
        out_shape=jax.ShapeDtypeStruct((n, d), jnp.bfloat16),
        compiler_params=pltpu.CompilerParams(dimension_semantics=("arbitrary",)),
        name="rmsnorm_in")(x, g)


def _pack_c_w(cw):
    z = jnp.zeros(cw.shape[:1] + cw.shape[2:], cw.dtype)
    half = lambda a, b: jnp.concatenate(
        [jnp.concatenate([a, z], axis=2), jnp.concatenate([z, b], axis=2)], axis=1)
    return jnp.concatenate([half(cw[:, 0], cw[:, 1]), half(cw[:, 2], cw[:, 3])], axis=2).astype(jnp.bfloat16)


def kernel(x, norm_g, w_in, conv_w, conv_b, a_ln_g, a_ln_b, a_pw, b_ln_g, b_ln_b, b_ws, b_bias,
           c_w, c_scale, w_out, final_g):
    bsz, seq, d = x.shape
    depth = w_in.shape[0]
    xf = x.reshape(bsz * seq, d)
    row = lambda v: v.reshape(1, -1)
    rows = lambda v: v[:, None, :]
    win, apw, wout = (w.astype(jnp.bfloat16) for w in (w_in, a_pw, w_out))
    cw = _pack_c_w(c_w)
    h = _rmsnorm_bf16(xf, row(norm_g[0]))
    for l in range(depth):
        cat = _mixer(h, win, apw, cw, conv_w, rows(conv_b), rows(a_ln_g), rows(a_ln_b),
                     rows(b_ln_g), rows(b_ln_b), b_ws, b_bias[..., None], rows(c_scale),
                     layer=l, seq=seq)
        if l + 1 < depth:
            xf, h = _outproj(cat, wout, xf, row(norm_g[l + 1]), layer=l, last=False)
        else:
            xf = _outproj(cat, wout, xf, row(final_g), layer=l, last=True)
    return xf.reshape(bsz, seq, d)
```

```python
import functools

import jax
import jax.numpy as jnp
from jax import lax
from jax.experimental import pallas as pl
from jax.experimental.pallas import tpu as pltpu

D_MODEL = 4096
HEAD_DIM = 128
W_A = 1536
W_B = 1280
W_C = 1280
H_B = W_B // HEAD_DIM
CONV_WIDTH = 31
CHUNK = 128
POOL_WINDOWS = (2, 4, 8, 16)
C_GROUP = W_C // len(POOL_WINDOWS)
EPS = 1e-6

COL_A_VAL = 0
COL_A_GLU = W_A
COL_A_GATE = 2 * W_A
COL_B_U = 3 * W_A
COL_B_V = COL_B_U + W_B
COL_B_GATE = COL_B_V + W_B
COL_C_IN = COL_B_GATE + W_B
COL_C_GATE = COL_C_IN + W_C

LANES = 128
SUBLANES = 8
VMEM_LIMIT_BYTES = 56 * 1024 * 1024

TOKEN_TILE = 512
SLAB_W = 512
W_BLOCK = 256
A_BLOCK = SLAB_W // 2
PAIR = 2 * HEAD_DIM
C_HALF = W_C // 2
CONV_HALO = 32
POOL_HALO = 16
GLU_ROWS = 128
LN_ROWS = 64
OUT_TM = 512
OUT_TN = 512
NORM_TM = 256


def _task_list():
    t = []
    for j in range(W_A // A_BLOCK):
        t.append(("avg", j, SLAB_W, [("a_val", j * A_BLOCK, A_BLOCK, 0),
                                     ("a_glu", j * A_BLOCK, A_BLOCK, A_BLOCK)]))
    for q in range(W_A // SLAB_W):
        t.append(("agate", q, SLAB_W, [("a_gate", q * SLAB_W, SLAB_W, 0)]))
    for s in range((W_B + W_C) // SLAB_W):
        lo, hi, parts = s * SLAB_W, (s + 1) * SLAB_W, []
        if lo < W_B:
            parts.append(("b_v", lo, min(hi, W_B) - lo, 0))
        if hi > W_B:
            start = max(lo, W_B)
            parts.append(("c_in", start - W_B, hi - start, start - lo))
        t.append(("bvc", s, SLAB_W, parts))
    for j in range(W_A // SLAB_W):
        t.append(("pw", j, SLAB_W, []))
    t.append(("cw", 0, 0, []))
    off = 0
    for j, w in enumerate((SLAB_W, SLAB_W, W_C - 2 * SLAB_W)):
        t.append(("cg", j, w, [("c_gate", off, w, 0)]))
        off += w
    for p in range(H_B // 2):
        t.append(("bug", p, 2 * PAIR, [("b_u", p * PAIR, PAIR, 0), ("b_gate", p * PAIR, PAIR, PAIR)]))
    assert len(t) % 2 == 0
    assert sum(1 for x in t if x[3]) % 2 == 0
    for x in t:
        assert all(c % W_BLOCK == 0 and w % W_BLOCK == 0 and d % W_BLOCK == 0 for _, c, w, d in x[3])
    return t


TASKS = _task_list()
GROUP_COL = {"a_val": COL_A_VAL, "a_glu": COL_A_GLU, "a_gate": COL_A_GATE, "b_u": COL_B_U,
             "b_v": COL_B_V, "b_gate": COL_B_GATE, "c_in": COL_C_IN, "c_gate": COL_C_GATE}
SLAB_TASK = [k for k, x in enumerate(TASKS) if x[3]]
TASK_SLAB = {k: s for s, k in enumerate(SLAB_TASK)}
LAST_BV = (W_B - 1) // SLAB_W
LAST_CIN = (W_B + W_C) // SLAB_W - 1


def _sigmoid(x):
    return 1.0 / (1.0 + jnp.exp(-x))


def _silu(x):
    return x * _sigmoid(x)


def _gelu(x):
    return jax.nn.gelu(x)


def _mixer_kernel(h_ref, win_hbm, apw_ref, cw_ref, convw_ref, convb_ref, alng_ref, alnb_ref,
                  blng_ref, blnb_ref, ws_ref, bbias_ref, cscale_ref,
                  out_ref,
                  wbuf, sem, rbuf0, rbuf1, cwb, ghalo, chalo, gext, gsh, ca, sgate, abf, gv, vnb, cext, dbf,
                  *, layer, tiles_per_seq):
    yc = gv
    rbufs = (rbuf0, rbuf1)
    i = pl.program_id(0)
    n_tiles = pl.num_programs(0)
    T = h_ref.shape[0]
    n_tasks, n_slabs = len(TASKS), len(SLAB_TASK)
    f32, bf16 = jnp.float32, jnp.bfloat16

    masks = {}

    def after(x, prev):
        if prev is None:
            return x
        if x.shape not in masks:
            masks[x.shape] = (lax.broadcasted_iota(jnp.int32, x.shape, 0) + i) >= 0
        return jnp.where(masks[x.shape], x, prev)

    def slab_copies(s):
        slot = s % 2
        return [pltpu.make_async_copy(
            win_hbm.at[layer, pl.ds((GROUP_COL[group] + col) // W_BLOCK, width // W_BLOCK)],
            wbuf.at[slot, pl.ds(dst // W_BLOCK, width // W_BLOCK)],
            sem.at[slot]) for group, col, width, dst in TASKS[SLAB_TASK[s]][3]]

    def start_slab(s):
        for c in slab_copies(s):
            c.start()

    def wait_slab(s):
        for c in slab_copies(s):
            c.wait()

    def issue_dot(k):
        kind, j, width, _ = TASKS[k]
        if kind == "cw":
            for q in range(2):
                cols = pl.ds(q * C_HALF, C_HALF)
                yc[:, cols] = jnp.dot(dbf[:, cols], cw_ref[:, cols], preferred_element_type=f32)
        elif kind == "pw":
            rbufs[k % 2][...] = jnp.dot(abf[...], apw_ref[:, pl.ds(j * SLAB_W, SLAB_W)],
                                        preferred_element_type=f32)
        else:
            slot = TASK_SLAB[k] % 2
            w = jnp.concatenate([wbuf[slot, b] for b in range(width // W_BLOCK)], axis=1)
            rbufs[k % 2][:, pl.ds(0, width)] = jnp.dot(h_ref[...], w, preferred_element_type=f32)

    def chained_rows(n_rows, fn):
        prev = None
        for r0 in range(0, T, n_rows):
            prev = fn(pl.ds(r0, n_rows), prev)

    def vec_avg(k, j):
        r = rbufs[k % 2]
        cols = pl.ds(j * A_BLOCK, A_BLOCK)
        gext[pl.ds(0, CONV_HALO), :] = ghalo[:, cols]
        for r0 in range(0, T, GLU_ROWS):
            rows = pl.ds(r0, GLU_ROWS)
            g = r[rows, pl.ds(0, A_BLOCK)] * _sigmoid(r[rows, pl.ds(A_BLOCK, A_BLOCK)])
            gext[pl.ds(CONV_HALO + r0, GLU_ROWS), :] = g
            if r0 + GLU_ROWS == T:
                ghalo[:, cols] = g[GLU_ROWS - CONV_HALO:, :]
        n_shift = T + CONV_HALO - SUBLANES
        for b in range(1, SUBLANES):
            gsh[b - 1, pl.ds(0, n_shift), :] = gext[pl.ds(b, n_shift), :]
        bias = jnp.broadcast_to(convb_ref[:, cols], (SUBLANES, A_BLOCK))
        acc = None
        for r0 in range(0, T, SUBLANES):
            acc = after(bias, acc)
            for tap in range(CONV_WIDTH):
                a, b = divmod(CONV_HALO - (CONV_WIDTH - 1) + tap, SUBLANES)
                src = gext if b == 0 else gsh.at[b - 1]
                acc = acc + (src[pl.ds(r0 + a * SUBLANES, SUBLANES), :]
                             * cwb[pl.ds(tap * SUBLANES, SUBLANES), cols])
            ca[pl.ds(r0, SUBLANES), cols] = acc

    def vec_agate(k, q):
        def group(rows, prev):
            y = _silu(after(rbufs[k % 2][rows, :], prev))
            sgate[rows, pl.ds(q * SLAB_W, SLAB_W)] = y
            return y
        chained_rows(SUBLANES, group)

    def layer_norm(xx, gamma, beta):
        mu = jnp.mean(xx, axis=-1, keepdims=True)
        xc = xx - mu
        var = jnp.mean(xc * xc, axis=-1, keepdims=True)
        return xc * lax.rsqrt(var + EPS) * gamma[...] + beta[...]

    def vec_ln_a():
        def group(rows, prev):
            y = _silu(layer_norm(after(ca[rows, :], prev), alng_ref, alnb_ref))
            abf[rows, :] = y.astype(bf16)
            return y
        chained_rows(LN_ROWS, group)

    def vec_ln_b():
        def group(rows, prev):
            y = layer_norm(after(gv[rows, :], prev), blng_ref, blnb_ref)
            vnb[rows, :] = y.astype(bf16)
            return y
        chained_rows(LN_ROWS, group)

    def vec_pw(k, j):
        cols = pl.ds(j * SLAB_W, SLAB_W)
        out_ref[:, cols] = (rbufs[k % 2][...] * sgate[:, cols]).astype(out_ref.dtype)

    def vec_bvc(k, s):
        for group_name, col, width, at in TASKS[k][3]:
            if group_name == "b_v":
                def group(rows, prev, col=col, width=width, at=at):
                    y = _gelu(after(rbufs[k % 2][rows, pl.ds(at, width)], prev))
                    gv[rows, pl.ds(col, width)] = y
                    return y
                chained_rows(SUBLANES, group)
            else:
                if col == 0:
                    cext[pl.ds(0, POOL_HALO), :] = chalo[...]
                cext[pl.ds(POOL_HALO, T), pl.ds(col, width)] = rbufs[k % 2][:, pl.ds(at, width)]

    def vec_bug(k, p):
        r = rbufs[k % 2]
        row_id = lax.broadcasted_iota(jnp.int32, (CHUNK, CHUNK), 0)
        col_id = lax.broadcasted_iota(jnp.int32, (CHUNK, CHUNK), 1)
        prev = None
        for hh in range(2):
            head = 2 * p + hh
            wsm = jnp.where(col_id <= row_id, ws_ref[head], 0.0).astype(bf16)
            for n in range(T // CHUNK):
                rows = pl.ds(n * CHUNK, CHUNK)
                u = after(r[rows, pl.ds(hh * HEAD_DIM, HEAD_DIM)], prev)
                sp = jnp.dot(wsm, vnb[rows, pl.ds(head * HEAD_DIM, HEAD_DIM)],
                             preferred_element_type=f32) + bbias_ref[head]
                prev = _gelu(u) * _silu(r[rows, pl.ds(PAIR + hh * HEAD_DIM, HEAD_DIM)]) * sp
                out_ref[rows, pl.ds(W_A + head * HEAD_DIM, HEAD_DIM)] = prev.astype(out_ref.dtype)

    def vec_pool():
        chalo[...] = cext[pl.ds(T, POOL_HALO), :]
        prev = None
        for b in range(W_C // LANES):
            lcols = pl.ds(b * LANES, LANES)
            lo_g = (b * LANES) // C_GROUP
            hi_g = (b * LANES + LANES - 1) // C_GROUP
            for r0 in range(0, T, LN_ROWS):
                pos = ((i % tiles_per_seq) * T + r0 + 1
                       + lax.broadcasted_iota(jnp.int32, (LN_ROWS, LANES), 0)).astype(f32)
                cur = after(cext[pl.ds(POOL_HALO + r0, LN_ROWS), lcols], prev)
                sums = {1: cur}
                run, have = cur, 1
                while have < POOL_WINDOWS[hi_g]:
                    for s in range(have, 2 * have):
                        run = run + cext[pl.ds(POOL_HALO + r0 - s, LN_ROWS), lcols]
                    have *= 2
                    sums[have] = run
                mean = sums[POOL_WINDOWS[hi_g]] / jnp.minimum(pos, float(POOL_WINDOWS[hi_g]))
                if lo_g != hi_g:
                    lane = lax.broadcasted_iota(jnp.int32, (LN_ROWS, LANES), 1) + b * LANES
                    mean_lo = sums[POOL_WINDOWS[lo_g]] / jnp.minimum(pos, float(POOL_WINDOWS[lo_g]))
                    mean = jnp.where(lane < hi_g * C_GROUP, mean_lo, mean)
                prev = mean - cur
                dbf[pl.ds(r0, LN_ROWS), lcols] = prev.astype(bf16)

    def vec_cg(k, j):
        _, off, w, _ = TASKS[k][3][0]
        cols = pl.ds(off, w)

        def group(rows, prev):
            y = yc[rows, cols] * cscale_ref[:, cols] * _silu(after(rbufs[k % 2][rows, pl.ds(0, w)], prev))
            out_ref[rows, pl.ds(W_A + W_B + off, w)] = y.astype(out_ref.dtype)
            return y
        chained_rows(SUBLANES, group)

    def vector_work(k):
        kind, j = TASKS[k][0], TASKS[k][1]
        if kind == "avg":
            vec_avg(k, j)
        elif kind == "agate":
            vec_agate(k, j)
            if j == 0:
                vec_ln_a()
        elif kind == "bvc":
            vec_bvc(k, j)
            if j == LAST_BV:
                vec_ln_b()
            if j == LAST_CIN:
                vec_pool()
        elif kind == "pw":
            vec_pw(k, j)
        elif kind == "cg":
            vec_cg(k, j)
        elif kind == "bug":
            vec_bug(k, j)

    @pl.when(i == 0)
    def _():
        start_slab(0)
        for tap in range(CONV_WIDTH):
            cwb[pl.ds(tap * SUBLANES, SUBLANES), :] = jnp.broadcast_to(
                convw_ref[pl.ds(tap, 1), :], (SUBLANES, W_A))

    @pl.when((i % tiles_per_seq) == 0)
    def _():
        ghalo[...] = jnp.zeros_like(ghalo)
        chalo[...] = jnp.zeros_like(chalo)

    def fetch_for(k):
        s = TASK_SLAB.get(k)
        if s is None:
            return
        if s + 1 < n_slabs:
            start_slab(s + 1)
        else:
            @pl.when(i + 1 < n_tiles)
            def _():
                start_slab(0)
        wait_slab(s)

    fetch_for(0)
    issue_dot(0)

    for k in range(n_tasks):
        if k + 1 < n_tasks:
            fetch_for(k + 1)
            issue_dot(k + 1)
        vector_work(k)


def _mixer(h, win, apw, cw, convw, convb, alng, alnb, blng, blnb, ws, bbias, cscale, *, layer, seq):
    n, d = h.shape
    T = TOKEN_TILE
    assert n % T == 0 and seq % T == 0 and T % CHUNK == 0 and d == D_MODEL
    assert W_B == W_C

    def per_layer(a):
        return pl.BlockSpec((None,) + a.shape[1:], lambda i: (layer,) + (0,) * (a.ndim - 1),
                            pipeline_mode=pl.Buffered(1))

    return pl.pallas_call(
        functools.partial(_mixer_kernel, layer=layer, tiles_per_seq=seq // T),
        grid=(n // T,),
        in_specs=[pl.BlockSpec((T, d), lambda i: (i, 0)), pl.BlockSpec(memory_space=pl.ANY),
                  per_layer(apw), per_layer(cw),
                  per_layer(convw), per_layer(convb), per_layer(alng), per_layer(alnb),
                  per_layer(blng), per_layer(blnb), per_layer(ws), per_layer(bbias), per_layer(cscale)],
        out_specs=pl.BlockSpec((T, d), lambda i: (i, 0)),
        out_shape=jax.ShapeDtypeStruct((n, d), jnp.bfloat16),
        scratch_shapes=[
            pltpu.VMEM((2, SLAB_W // W_BLOCK, D_MODEL, W_BLOCK), jnp.bfloat16),
            pltpu.SemaphoreType.DMA((2,)),
            pltpu.VMEM((T, SLAB_W), jnp.float32),
            pltpu.VMEM((T, SLAB_W), jnp.float32),
            pltpu.VMEM((CONV_WIDTH * SUBLANES, W_A), jnp.float32),
            pltpu.VMEM((CONV_HALO, W_A), jnp.float32),
            pltpu.VMEM((POOL_HALO, W_C), jnp.float32),
            pltpu.VMEM((CONV_HALO + T, A_BLOCK), jnp.float32),
            pltpu.VMEM((SUBLANES - 1, CONV_HALO + T, A_BLOCK), jnp.float32),
            pltpu.VMEM((T, W_A), jnp.float32),
            pltpu.VMEM((T, W_A), jnp.float32),
            pltpu.VMEM((T, W_A), jnp.bfloat16),
            pltpu.VMEM((T, W_B), jnp.float32),
            pltpu.VMEM((T, W_B), jnp.bfloat16),
            pltpu.VMEM((POOL_HALO + T, W_C), jnp.float32),
            pltpu.VMEM((T, W_C), jnp.bfloat16),
        ],
        compiler_params=pltpu.CompilerParams(
            dimension_semantics=("arbitrary",), vmem_limit_bytes=VMEM_LIMIT_BYTES),
        name="mixer",
    )(h, win, apw, cw, convw, convb, alng, alnb, blng, blnb, ws, bbias, cscale)


def _rms(xx, g):
    r = lax.rsqrt(jnp.mean(xx * xx, axis=-1, keepdims=True) + EPS)
    return xx * r * g


def _outproj_mid_kernel(c_ref, w_ref, x_ref, g_ref, xnew_ref, h_ref):
    j = pl.program_id(1)
    tn = w_ref.shape[1]
    y = jnp.dot(c_ref[...], w_ref[...], preferred_element_type=jnp.float32) + x_ref[...]
    xnew_ref[:, pl.ds(pl.multiple_of(j * tn, tn), tn)] = y

    @pl.when(j == pl.num_programs(1) - 1)
    def _():
        h_ref[...] = _rms(xnew_ref[...], g_ref[...]).astype(h_ref.dtype)


def _outproj_last_kernel(c_ref, w_ref, x_ref, g_ref, out_ref):
    j = pl.program_id(1)
    tn = w_ref.shape[1]
    y = jnp.dot(c_ref[...], w_ref[...], preferred_element_type=jnp.float32) + x_ref[...]
    out_ref[:, pl.ds(pl.multiple_of(j * tn, tn), tn)] = y

    @pl.when(j == pl.num_programs(1) - 1)
    def _():
        out_ref[...] = _rms(out_ref[...], g_ref[...])


def _outproj(c, w, x, g, *, layer, last):
    n, d = x.shape
    tm, tn = OUT_TM, OUT_TN
    assert n % tm == 0 and d % tn == 0 and w.shape[1:] == (d // tn, d, tn)
    in_specs = [pl.BlockSpec((tm, d), lambda i, j: (i, 0)),
                pl.BlockSpec((None, None, d, tn), lambda i, j: (layer, j, 0, 0)),
                pl.BlockSpec((tm, tn), lambda i, j: (i, j)),
                pl.BlockSpec((1, d), lambda i, j: (0, 0))]
    row_block = pl.BlockSpec((tm, d), lambda i, j: (i, 0))
    params = pltpu.CompilerParams(
        dimension_semantics=("arbitrary", "arbitrary"), vmem_limit_bytes=VMEM_LIMIT_BYTES)
    if last:
        return pl.pallas_call(
            _outproj_last_kernel, grid=(n // tm, d // tn), in_specs=in_specs,
            out_specs=row_block, out_shape=jax.ShapeDtypeStruct((n, d), jnp.float32),
            compiler_params=params, name="outproj_last")(c, w, x, g)
    return pl.pallas_call(
        _outproj_mid_kernel, grid=(n // tm, d // tn), in_specs=in_specs,
        out_specs=(row_block, row_block),
        out_shape=(jax.ShapeDtypeStruct((n, d), jnp.float32),
                   jax.ShapeDtypeStruct((n, d), jnp.bfloat16)),
        compiler_params=params, name="outproj_mid")(c, w, x, g)


def _rmsnorm_kernel(x_ref, g_ref, h_ref):
    h_ref[...] = _rms(x_ref[...], g_ref[...]).astype(h_ref.dtype)


def _rmsnorm_bf16(x, g):
    n, d = x.shape
    assert n % NORM_TM == 0
    return pl.pallas_call(
        _rmsnorm_kernel, grid=(n // NORM_TM,),
        in_specs=[pl.BlockSpec((NORM_TM, d), lambda i: (i, 0)),
                  pl.BlockSpec((1, d), lambda i: (0, 0))],
        out_specs=pl.BlockSpec((NORM_TM, d), lambda i: (i, 0)),
        out_shape=jax.ShapeDtypeStruct((n, d), jnp.bfloat16),
        compiler_params=pltpu.CompilerParams(dimension_semantics=("arbitrary",)),
        name="rmsnorm_in")(x, g)


def _column_blocks(w, width):
    depth, rows, cols = w.shape
    return w.astype(jnp.bfloat16).reshape(depth, rows, cols // width, width).transpose(0, 2, 1, 3)


def _pack_c_w(cw):
    z = jnp.zeros(cw.shape[:1] + cw.shape[2:], cw.dtype)
    half = lambda a, b: jnp.concatenate(
        [jnp.concatenate([a, z], axis=2), jnp.concatenate([z, b], axis=2)], axis=1)
    return jnp.concatenate([half(cw[:, 0], cw[:, 1]), half(cw[:, 2], cw[:, 3])], axis=2).astype(jnp.bfloat16)


def kernel(x, norm_g, w_in, conv_w, conv_b, a_ln_g, a_ln_b, a_pw, b_ln_g, b_ln_b, b_ws, b_bias,
           c_w, c_scale, w_out, final_g):
    bsz, seq, d = x.shape
    depth = w_in.shape[0]
    xf = x.reshape(bsz * seq, d)
    row = lambda v: v.reshape(1, -1)
    rows = lambda v: v[:, None, :]
    win = _column_blocks(w_in, W_BLOCK)
    wout = _column_blocks(w_out, OUT_TN)
    apw = a_pw.astype(jnp.bfloat16)
    cw = _pack_c_w(c_w)
    h = _rmsnorm_bf16(xf, row(norm_g[0]))
    for l in range(depth):
        cat = _mixer(h, win, apw, cw, conv_w, rows(conv_b), rows(a_ln_g), rows(a_ln_b),
                     rows(b_ln_g), rows(b_ln_b), b_ws, b_bias[..., None], rows(c_scale),
                     layer=l, seq=seq)
        if l + 1 < depth:
            xf, h = _outproj(cat, wout, xf, row(norm_g[l + 1]), layer=l, last=False)
        else:
            xf = _outproj(cat, wout, xf, row(final_g), layer=l, last=True)
    return xf.reshape(bsz, seq, d)
```

```python
import functools

import jax
import jax.numpy as jnp
from jax import lax
from jax.experimental import pallas as pl
from jax.experimental.pallas import tpu as pltpu

D_MODEL = 4096
HEAD_DIM = 128
W_A = 1536
W_B = 1280
W_C = 1280
H_B = W_B // HEAD_DIM
CONV_WIDTH = 31
CHUNK = 128
POOL_WINDOWS = (2, 4, 8, 16)
C_GROUP = W_C // len(POOL_WINDOWS)
EPS = 1e-6

COL_A_VAL = 0
COL_A_GLU = W_A
COL_A_GATE = 2 * W_A
COL_B_U = 3 * W_A
COL_B_V = COL_B_U + W_B
COL_B_GATE = COL_B_V + W_B
COL_C_IN = COL_B_GATE + W_B
COL_C_GATE = COL_C_IN + W_C

LANES = 128
SUBLANES = 8
VMEM_LIMIT_BYTES = 56 * 1024 * 1024

TOKEN_TILE = 512
SLAB_W = 512
A_BLOCK = SLAB_W // 2
PAIR = 2 * HEAD_DIM
C_HALF = W_C // 2
CONV_HALO = 32
POOL_HALO = 16
GLU_ROWS = 128
LN_ROWS = 64
OUT_TM = 512
OUT_TN = 512
NORM_TM = 256


def _task_list():
    t = []
    for j in range(W_A // A_BLOCK):
        t.append(("avg", j, SLAB_W, [("a_val", j * A_BLOCK, A_BLOCK, 0),
                                     ("a_glu", j * A_BLOCK, A_BLOCK, A_BLOCK)]))
    for q in range(W_A // SLAB_W):
        t.append(("agate", q, SLAB_W, [("a_gate", q * SLAB_W, SLAB_W, 0)]))
    for s in range((W_B + W_C) // SLAB_W):
        lo, hi, parts = s * SLAB_W, (s + 1) * SLAB_W, []
        if lo < W_B:
            parts.append(("b_v", lo, min(hi, W_B) - lo, 0))
        if hi > W_B:
            start = max(lo, W_B)
            parts.append(("c_in", start - W_B, hi - start, start - lo))
        t.append(("bvc", s, SLAB_W, parts))
    for j in range(W_A // SLAB_W):
        t.append(("pw", j, SLAB_W, []))
    t.append(("cw", 0, 0, []))
    off = 0
    for j, w in enumerate((SLAB_W, SLAB_W, W_C - 2 * SLAB_W)):
        t.append(("cg", j, w, [("c_gate", off, w, 0)]))
        off += w
    for p in range(H_B // 2):
        t.append(("bug", p, 2 * PAIR, [("b_u", p * PAIR, PAIR, 0), ("b_gate", p * PAIR, PAIR, PAIR)]))
    assert len(t) % 2 == 0
    assert sum(1 for x in t if x[3]) % 2 == 0
    return t


TASKS = _task_list()
GROUP_COL = {"a_val": COL_A_VAL, "a_glu": COL_A_GLU, "a_gate": COL_A_GATE, "b_u": COL_B_U,
             "b_v": COL_B_V, "b_gate": COL_B_GATE, "c_in": COL_C_IN, "c_gate": COL_C_GATE}
SLAB_TASK = [k for k, x in enumerate(TASKS) if x[3]]
TASK_SLAB = {k: s for s, k in enumerate(SLAB_TASK)}
LAST_BV = (W_B - 1) // SLAB_W
LAST_CIN = (W_B + W_C) // SLAB_W - 1


def _sigmoid(x):
    return 1.0 / (1.0 + jnp.exp(-x))


def _silu(x):
    return x * _sigmoid(x)


def _gelu(x):
    return jax.nn.gelu(x)


def _mixer_kernel(h_ref, win_hbm, apw_ref, cw_ref, convw_ref, convb_ref, alng_ref, alnb_ref,
                  blng_ref, blnb_ref, ws_ref, bbias_ref, cscale_ref,
                  out_ref,
                  wbuf, sem, rbuf0, rbuf1, cwb, ghalo, chalo, gext, gsh, ca, sgate, abf, gv, vnb, cext, dbf,
                  *, layer, tiles_per_seq):
    yc = gv
    rbufs = (rbuf0, rbuf1)
    i = pl.program_id(0)
    n_tiles = pl.num_programs(0)
    T = h_ref.shape[0]
    n_tasks, n_slabs = len(TASKS), len(SLAB_TASK)
    f32, bf16 = jnp.float32, jnp.bfloat16

    masks = {}

    def after(x, prev):
        if prev is None:
            return x
        if x.shape not in masks:
            masks[x.shape] = (lax.broadcasted_iota(jnp.int32, x.shape, 0) + i) >= 0
        return jnp.where(masks[x.shape], x, prev)

    def slab_copies(s):
        slot = s % 2
        return [pltpu.make_async_copy(
            win_hbm.at[layer, :, pl.ds(GROUP_COL[group] + col, width)],
            wbuf.at[slot, :, pl.ds(dst, width)],
            sem.at[slot]) for group, col, width, dst in TASKS[SLAB_TASK[s]][3]]

    def start_slab(s):
        for c in slab_copies(s):
            c.start()

    def wait_slab(s):
        for c in slab_copies(s):
            c.wait()

    def issue_dot(k):
        kind, j, width, _ = TASKS[k]
        if kind == "cw":
            for q in range(2):
                cols = pl.ds(q * C_HALF, C_HALF)
                yc[:, cols] = jnp.dot(dbf[:, cols], cw_ref[:, cols], preferred_element_type=f32)
        elif kind == "pw":
            rbufs[k % 2][...] = jnp.dot(abf[...], apw_ref[:, pl.ds(j * SLAB_W, SLAB_W)],
                                        preferred_element_type=f32)
        else:
            w = wbuf[TASK_SLAB[k] % 2, :, pl.ds(0, width)]
            n_row_parts = 1 if width >= SLAB_W else 2
            for rows in (pl.ds(r0, T // n_row_parts) for r0 in range(0, T, T // n_row_parts)):
                rbufs[k % 2][rows, pl.ds(0, width)] = jnp.dot(h_ref[rows, :], w, preferred_element_type=f32)

    def chained_rows(n_rows, fn, lo=0, hi=None, n_chains=1):
        prev = [None] * n_chains
        for g, r0 in enumerate(range(lo, T if hi is None else hi, n_rows)):
            prev[g % n_chains] = fn(pl.ds(r0, n_rows), prev[g % n_chains])

    def vec_avg(k, j):
        r = rbufs[k % 2]
        cols = pl.ds(j * A_BLOCK, A_BLOCK)
        gext[pl.ds(0, CONV_HALO), :] = ghalo[:, cols]
        for r0 in range(0, T, GLU_ROWS):
            rows = pl.ds(r0, GLU_ROWS)
            g = r[rows, pl.ds(0, A_BLOCK)] * _sigmoid(r[rows, pl.ds(A_BLOCK, A_BLOCK)])
            gext[pl.ds(CONV_HALO + r0, GLU_ROWS), :] = g
            if r0 + GLU_ROWS == T:
                ghalo[:, cols] = g[GLU_ROWS - CONV_HALO:, :]
        n_shift = T + CONV_HALO - SUBLANES
        for b in range(1, SUBLANES):
            gsh[b - 1, pl.ds(0, n_shift), :] = gext[pl.ds(b, n_shift), :]
        bias = jnp.broadcast_to(convb_ref[:, cols], (SUBLANES, A_BLOCK))
        acc = None
        for r0 in range(0, T, SUBLANES):
            acc = after(bias, acc)
            for tap in range(CONV_WIDTH):
                a, b = divmod(CONV_HALO - (CONV_WIDTH - 1) + tap, SUBLANES)
                src = gext if b == 0 else gsh.at[b - 1]
                acc = acc + (src[pl.ds(r0 + a * SUBLANES, SUBLANES), :]
                             * cwb[pl.ds(tap * SUBLANES, SUBLANES), cols])
            ca[pl.ds(r0, SUBLANES), cols] = acc

    def vec_agate(k, q):
        def group(rows, prev):
            y = _silu(after(rbufs[k % 2][rows, :], prev))
            sgate[rows, pl.ds(q * SLAB_W, SLAB_W)] = y
            return y
        chained_rows(SUBLANES, group)

    def layer_norm(xx, gamma, beta):
        mu = jnp.mean(xx, axis=-1, keepdims=True)
        xc = xx - mu
        var = jnp.mean(xc * xc, axis=-1, keepdims=True)
        return xc * lax.rsqrt(var + EPS) * gamma[...] + beta[...]

    def vec_ln_a(lo, hi):
        def group(rows, prev):
            y = _silu(layer_norm(after(ca[rows, :], prev), alng_ref, alnb_ref))
            abf[rows, :] = y.astype(bf16)
            return y
        chained_rows(LN_ROWS, group, lo, hi)

    def vec_ln_b():
        def group(rows, prev):
            y = layer_norm(after(gv[rows, :], prev), blng_ref, blnb_ref)
            vnb[rows, :] = y.astype(bf16)
            return y
        chained_rows(LN_ROWS, group)

    def vec_pw(k, j):
        cols = pl.ds(j * SLAB_W, SLAB_W)
        out_ref[:, cols] = (rbufs[k % 2][...] * sgate[:, cols]).astype(out_ref.dtype)

    def vec_bvc(k, s):
        for group_name, col, width, at in TASKS[k][3]:
            if group_name == "b_v":
                def group(rows, prev, col=col, width=width, at=at):
                    y = _gelu(after(rbufs[k % 2][rows, pl.ds(at, width)], prev))
                    gv[rows, pl.ds(col, width)] = y
                    return y
                chained_rows(SUBLANES, group)
            else:
                if col == 0:
                    cext[pl.ds(0, POOL_HALO), :] = chalo[...]
                cext[pl.ds(POOL_HALO, T), pl.ds(col, width)] = rbufs[k % 2][:, pl.ds(at, width)]

    def vec_bug(k, p):
        r = rbufs[k % 2]
        row_id = lax.broadcasted_iota(jnp.int32, (CHUNK, CHUNK), 0)
        col_id = lax.broadcasted_iota(jnp.int32, (CHUNK, CHUNK), 1)
        for hh in range(2):
            head = 2 * p + hh
            prev = None
            wsm = jnp.where(col_id <= row_id, ws_ref[head], 0.0).astype(bf16)
            for n in range(T // CHUNK):
                rows = pl.ds(n * CHUNK, CHUNK)
                u = after(r[rows, pl.ds(hh * HEAD_DIM, HEAD_DIM)], prev)
                sp = jnp.dot(wsm, vnb[rows, pl.ds(head * HEAD_DIM, HEAD_DIM)],
                             preferred_element_type=f32) + bbias_ref[head]
                prev = _gelu(u) * _silu(r[rows, pl.ds(PAIR + hh * HEAD_DIM, HEAD_DIM)]) * sp
                out_ref[rows, pl.ds(W_A + head * HEAD_DIM, HEAD_DIM)] = prev.astype(out_ref.dtype)

    def vec_pool():
        chalo[...] = cext[pl.ds(T, POOL_HALO), :]
        prev = None
        for b in range(W_C // LANES):
            lcols = pl.ds(b * LANES, LANES)
            lo_g = (b * LANES) // C_GROUP
            hi_g = (b * LANES + LANES - 1) // C_GROUP
            for r0 in range(0, T, LN_ROWS):
                pos = ((i % tiles_per_seq) * T + r0 + 1
                       + lax.broadcasted_iota(jnp.int32, (LN_ROWS, LANES), 0)).astype(f32)
                cur = after(cext[pl.ds(POOL_HALO + r0, LN_ROWS), lcols], prev)
                sums = {1: cur}
                run, have = cur, 1
                while have < POOL_WINDOWS[hi_g]:
                    for s in range(have, 2 * have):
                        run = run + cext[pl.ds(POOL_HALO + r0 - s, LN_ROWS), lcols]
                    have *= 2
                    sums[have] = run
                mean = sums[POOL_WINDOWS[hi_g]] / jnp.minimum(pos, float(POOL_WINDOWS[hi_g]))
                if lo_g != hi_g:
                    lane = lax.broadcasted_iota(jnp.int32, (LN_ROWS, LANES), 1) + b * LANES
                    mean_lo = sums[POOL_WINDOWS[lo_g]] / jnp.minimum(pos, float(POOL_WINDOWS[lo_g]))
                    mean = jnp.where(lane < hi_g * C_GROUP, mean_lo, mean)
                prev = mean - cur
                dbf[pl.ds(r0, LN_ROWS), lcols] = prev.astype(bf16)

    def vec_cg(k, j):
        _, off, w, _ = TASKS[k][3][0]
        cols = pl.ds(off, w)

        def group(rows, prev):
            y = yc[rows, cols] * cscale_ref[:, cols] * _silu(after(rbufs[k % 2][rows, pl.ds(0, w)], prev))
            out_ref[rows, pl.ds(W_A + W_B + off, w)] = y.astype(out_ref.dtype)
            return y
        chained_rows(SUBLANES, group)

    def vector_work(k):
        kind, j = TASKS[k][0], TASKS[k][1]
        if kind == "avg":
            vec_avg(k, j)
        elif kind == "agate":
            vec_agate(k, j)
            if j < 2:
                vec_ln_a(j * (T // 2), (j + 1) * (T // 2))
        elif kind == "bvc":
            vec_bvc(k, j)
            if j == LAST_BV:
                vec_ln_b()
            if j == LAST_CIN:
                vec_pool()
        elif kind == "pw":
            vec_pw(k, j)
        elif kind == "cg":
            vec_cg(k, j)
        elif kind == "bug":
            vec_bug(k, j)

    @pl.when(i == 0)
    def _():
        start_slab(0)
        for tap in range(CONV_WIDTH):
            cwb[pl.ds(tap * SUBLANES, SUBLANES), :] = jnp.broadcast_to(
                convw_ref[pl.ds(tap, 1), :], (SUBLANES, W_A))

    @pl.when((i % tiles_per_seq) == 0)
    def _():
        ghalo[...] = jnp.zeros_like(ghalo)
        chalo[...] = jnp.zeros_like(chalo)

    def fetch_for(k):
        s = TASK_SLAB.get(k)
        if s is None:
            return
        if s + 1 < n_slabs:
            start_slab(s + 1)
        else:
            @pl.when(i + 1 < n_tiles)
            def _():
                start_slab(0)
        wait_slab(s)

    fetch_for(0)
    issue_dot(0)

    for k in range(n_tasks):
        if k + 1 < n_tasks:
            fetch_for(k + 1)
            issue_dot(k + 1)
        vector_work(k)


def _mixer(h, win, apw, cw, convw, convb, alng, alnb, blng, blnb, ws, bbias, cscale, *, layer, seq):
    n, d = h.shape
    T = TOKEN_TILE
    assert n % T == 0 and seq % T == 0 and T % CHUNK == 0 and d == D_MODEL
    assert W_B == W_C

    def per_layer(a):
        return pl.BlockSpec((None,) + a.shape[1:], lambda i: (layer,) + (0,) * (a.ndim - 1),
                            pipeline_mode=pl.Buffered(1))

    return pl.pallas_call(
        functools.partial(_mixer_kernel, layer=layer, tiles_per_seq=seq // T),
        grid=(n // T,),
        in_specs=[pl.BlockSpec((T, d), lambda i: (i, 0)), pl.BlockSpec(memory_space=pl.ANY),
                  per_layer(apw), per_layer(cw),
                  per_layer(convw), per_layer(convb), per_layer(alng), per_layer(alnb),
                  per_layer(blng), per_layer(blnb), per_layer(ws), per_layer(bbias), per_layer(cscale)],
        out_specs=pl.BlockSpec((T, d), lambda i: (i, 0)),
        out_shape=jax.ShapeDtypeStruct((n, d), jnp.bfloat16),
        scratch_shapes=[
            pltpu.VMEM((2, D_MODEL, SLAB_W), jnp.bfloat16),
            pltpu.SemaphoreType.DMA((2,)),
            pltpu.VMEM((T, SLAB_W), jnp.float32),
            pltpu.VMEM((T, SLAB_W), jnp.float32),
            pltpu.VMEM((CONV_WIDTH * SUBLANES, W_A), jnp.float32),
            pltpu.VMEM((CONV_HALO, W_A), jnp.float32),
            pltpu.VMEM((POOL_HALO, W_C), jnp.float32),
            pltpu.VMEM((CONV_HALO + T, A_BLOCK), jnp.float32),
            pltpu.VMEM((SUBLANES - 1, CONV_HALO + T, A_BLOCK), jnp.float32),
            pltpu.VMEM((T, W_A), jnp.float32),
            pltpu.VMEM((T, W_A), jnp.float32),
            pltpu.VMEM((T, W_A), jnp.bfloat16),
            pltpu.VMEM((T, W_B), jnp.float32),
            pltpu.VMEM((T, W_B), jnp.bfloat16),
            pltpu.VMEM((POOL_HALO + T, W_C), jnp.float32),
            pltpu.VMEM((T, W_C), jnp.bfloat16),
        ],
        compiler_params=pltpu.CompilerParams(
            dimension_semantics=("arbitrary",), vmem_limit_bytes=VMEM_LIMIT_BYTES),
        name="mixer",
    )(h, win, apw, cw, convw, convb, alng, alnb, blng, blnb, ws, bbias, cscale)


def _rms(xx, g):
    r = lax.rsqrt(jnp.mean(xx * xx, axis=-1, keepdims=True) + EPS)
    return xx * r * g


def _outproj_mid_kernel(c_ref, w_ref, x_ref, g_ref, xnew_ref, h_ref):
    j = pl.program_id(1)
    tn = w_ref.shape[1]
    y = jnp.dot(c_ref[...], w_ref[...], preferred_element_type=jnp.float32) + x_ref[...]
    xnew_ref[:, pl.ds(pl.multiple_of(j * tn, tn), tn)] = y

    @pl.when(j == pl.num_programs(1) - 1)
    def _():
        h_ref[...] = _rms(xnew_ref[...], g_ref[...]).astype(h_ref.dtype)


def _outproj_last_kernel(c_ref, w_ref, x_ref, g_ref, out_ref):
    j = pl.program_id(1)
    tn = w_ref.shape[1]
    y = jnp.dot(c_ref[...], w_ref[...], preferred_element_type=jnp.float32) + x_ref[...]
    out_ref[:, pl.ds(pl.multiple_of(j * tn, tn), tn)] = y

    @pl.when(j == pl.num_programs(1) - 1)
    def _():
        out_ref[...] = _rms(out_ref[...], g_ref[...])


def _outproj(c, w, x, g, *, layer, last):
    n, d = x.shape
    tm, tn = OUT_TM, OUT_TN
    assert n % tm == 0 and d % tn == 0
    in_specs = [pl.BlockSpec((tm, d), lambda i, j: (i, 0)),
                pl.BlockSpec((None, d, tn), lambda i, j: (layer, 0, j)),
                pl.BlockSpec((tm, tn), lambda i, j: (i, j)),
                pl.BlockSpec((1, d), lambda i, j: (0, 0))]
    row_block = pl.BlockSpec((tm, d), lambda i, j: (i, 0))
    params = pltpu.CompilerParams(
        dimension_semantics=("arbitrary", "arbitrary"), vmem_limit_bytes=VMEM_LIMIT_BYTES)
    if last:
        return pl.pallas_call(
            _outproj_last_kernel, grid=(n // tm, d // tn), in_specs=in_specs,
            out_specs=row_block, out_shape=jax.ShapeDtypeStruct((n, d), jnp.float32),
            compiler_params=params, name="outproj_last")(c, w, x, g)
    return pl.pallas_call(
        _outproj_mid_kernel, grid=(n // tm, d // tn), in_specs=in_specs,
        out_specs=(row_block, row_block),
        out_shape=(jax.ShapeDtypeStruct((n, d), jnp.float32),
                   jax.ShapeDtypeStruct((n, d), jnp.bfloat16)),
        compiler_params=params, name="outproj_mid")(c, w, x, g)


def _rmsnorm_kernel(x_ref, g_ref, h_ref):
    h_ref[...] = _rms(x_ref[...], g_ref[...]).astype(h_ref.dtype)


def _rmsnorm_bf16(x, g):
    n, d = x.shape
    assert n % NORM_TM == 0
    return pl.pallas_call(
        _rmsnorm_kernel, grid=(n // NORM_TM,),
        in_specs=[pl.BlockSpec((NORM_TM, d), lambda i: (i, 0)),
                  pl.BlockSpec((1, d), lambda i: (0, 0))],
        out_specs=pl.BlockSpec((NORM_TM, d), lambda i: (i, 0)),
        out_shape=jax.ShapeDtypeStruct((n, d), jnp.bfloat16),
        compiler_params=pltpu.CompilerParams(dimension_semantics=("arbitrary",)),
        name="rmsnorm_in")(x, g)


def _pack_c_w(cw):
    z = jnp.zeros(cw.shape[:1] + cw.shape[2:], cw.dtype)
    half = lambda a, b: jnp.concatenate(
        [jnp.concatenate([a, z], axis=2), jnp.concatenate([z, b], axis=2)], axis=1)
    return jnp.concatenate([half(cw[:, 0], cw[:, 1]), half(cw[:, 2], cw[:, 3])], axis=2).astype(jnp.bfloat16)


def kernel(x, norm_g, w_in, conv_w, conv_b, a_ln_g, a_ln_b, a_pw, b_ln_g, b_ln_b, b_ws, b_bias,
           c_w, c_scale, w_out, final_g):
    bsz, seq, d = x.shape
    depth = w_in.shape[0]
    xf = x.reshape(bsz * seq, d)
    row = lambda v: v.reshape(1, -1)
    rows = lambda v: v[:, None, :]
    win, apw, wout = (w.astype(jnp.bfloat16) for w in (w_in, a_pw, w_out))
    cw = _pack_c_w(c_w)
    h = _rmsnorm_bf16(xf, row(norm_g[0]))
    for l in range(depth):
        cat = _mixer(h, win, apw, cw, conv_w, rows(conv_b), rows(a_ln_g), rows(a_ln_b),
                     rows(b_ln_g), rows(b_ln_b), b_ws, b_bias[..., None], rows(c_scale),
                     layer=l, seq=seq)
        if l + 1 < depth:
            xf, h = _outproj(cat, wout, xf, row(norm_g[l + 1]), layer=l, last=False)
        else:
            xf = _outproj(cat, wout, xf, row(final_g), layer=l, last=True)
    return xf.reshape(bsz, seq, d)
```
